```python
import jax, jax.numpy as jnp
from jax import lax
import numpy as np

D_MODEL = 2048
BATCH = 4
SEQ = 2048
DEPTH = 4
DEC_BATCH = 32
DEC_SEQ = 8
PAST_LEN = 16384
PAGE_SIZE = 128

N_A_LAYERS = DEPTH // 2
N_B_LAYERS = DEPTH - N_A_LAYERS
POOL_WINDOWS = (2, 4, 8, 16)
N_POOL_GROUPS = len(POOL_WINDOWS)
POOL_GROUP = D_MODEL // N_POOL_GROUPS
POOL_STATE = max(POOL_WINDOWS) - 1
HEAD_DIM = 64
N_HEADS = D_MODEL // HEAD_DIM
N_KV_HEADS = N_HEADS // 8
GQA_GROUP = N_HEADS // N_KV_HEADS
WINDOW = 128
BLOCK = 128
D_FF = 4 * D_MODEL
RMS_EPS = 1e-5

kernel_name = "yoco_pool_swa_sink_decoder_step"


def _rmsnorm(x, g):
    xf = x.astype(jnp.float32)
    y = xf * lax.rsqrt(jnp.mean(xf * xf, axis=-1, keepdims=True) + RMS_EPS)
    return (y * g.astype(jnp.float32)).astype(x.dtype)


def _pool_mix(u, past, pos0, w_pool, scale):
    T = u.shape[1]
    ext = jnp.concatenate([past.astype(u.dtype), u], axis=1)
    c = jnp.cumsum(ext.astype(jnp.float32), axis=1)
    c = jnp.pad(c, ((0, 0), (1, 0), (0, 0)))
    pos = pos0 + jnp.arange(T)
    uf = u.astype(jnp.float32)
    outs = []
    for g, w in enumerate(POOL_WINDOWS):
        sl = slice(g * POOL_GROUP, (g + 1) * POOL_GROUP)
        s = c[:, POOL_STATE + 1:, sl] - c[:, POOL_STATE + 1 - w:POOL_STATE + 1 - w + T, sl]
        cnt = jnp.minimum(pos + 1, w).astype(jnp.float32)
        d = s / cnt[None, :, None] - uf[:, :, sl]
        outs.append(jnp.einsum('btc,ce->bte', d.astype(u.dtype), w_pool[g]))
    out = jnp.concatenate(outs, axis=-1) * scale
    return out.astype(u.dtype), ext[:, -POOL_STATE:]


def _banded_sink_attention(q, k, v, q_pos, k_pos, sinks):
    B, N, Q = q.shape[:3]
    qg = q.reshape(B, N, Q, N_KV_HEADS, GQA_GROUP, HEAD_DIM)
    s = jnp.einsum('bnqkgd,bnskd->bnkgqs', qg, k,
                   preferred_element_type=jnp.float32) * (HEAD_DIM ** -0.5)
    rel = q_pos[:, :, None] - k_pos[:, None, :]
    allowed = (k_pos[:, None, :] >= 0) & (rel >= 0) & (rel < WINDOW)
    s = jnp.where(allowed[None, :, None, None], s, -jnp.inf)
    sk = sinks.astype(jnp.float32).reshape(N_KV_HEADS, GQA_GROUP)[None, None, :, :, None, None]
    m = jnp.maximum(jnp.max(s, axis=-1, keepdims=True), sk)
    p = jnp.exp(s - m)
    p = p / (jnp.sum(p, axis=-1, keepdims=True) + jnp.exp(sk - m))
    o = jnp.einsum('bnkgqs,bnskd->bnqkgd', p.astype(v.dtype), v)
    return o.reshape(B, N * Q, N_HEADS * HEAD_DIM)


def _attn_prompt(hn, k_sh, v_sh, w_q, w_o, sinks):
    B, T, _ = hn.shape
    nb = T // BLOCK
    q = (hn @ w_q).reshape(B, nb, BLOCK, N_HEADS, HEAD_DIM)

    def with_prev(a):
        a = a.reshape(B, nb, BLOCK, N_KV_HEADS, HEAD_DIM)
        prev = jnp.concatenate([jnp.zeros_like(a[:, :1]), a[:, :-1]], axis=1)
        return jnp.concatenate([prev, a], axis=2)

    blk = jnp.arange(nb)[:, None]
    q_pos = blk * BLOCK + jnp.arange(BLOCK)[None, :]
    k_pos = (blk - 1) * BLOCK + jnp.arange(2 * BLOCK)[None, :]
    o = _banded_sink_attention(q, with_prev(k_sh), with_prev(v_sh), q_pos, k_pos, sinks)
    return o @ w_o


def _attn_sample(hn, k_new, v_new, k_buf, v_buf, pos0, w_q, w_o, sinks):
    B, T, _ = hn.shape
    wb = k_buf.shape[1]
    q = (hn @ w_q).reshape(B, 1, T, N_HEADS, HEAD_DIM)
    kk = jnp.concatenate([k_buf.astype(k_new.dtype), k_new], axis=1)[:, None]
    vv = jnp.concatenate([v_buf.astype(v_new.dtype), v_new], axis=1)[:, None]
    new_pos = pos0 + jnp.arange(T)
    k_pos = jnp.concatenate([pos0 - wb + jnp.arange(wb), new_pos])[None]
    o = _banded_sink_attention(q, kk, vv, new_pos[None], k_pos, sinks)
    return o @ w_o


def _sq_relu_mlp(hn, w_up, w_down):
    a = jax.nn.relu(hn @ w_up)
    return (a * a) @ w_down


def _trunk(x, pool_past, k_buf, v_buf, pos0, norm_a, w_pool, pool_scale, norm_kv, w_k, w_v,
           norm_b, w_q, w_o, sinks, norm_mlp, w_up, w_down, norm_f):
    B, T, _ = x.shape
    wb = min(WINDOW, PAST_LEN)
    h = x
    new_pool = []
    k_sh = v_sh = None
    for l in range(DEPTH):
        if l < N_A_LAYERS:
            out, st = _pool_mix(_rmsnorm(h, norm_a[l]), pool_past[l], pos0, w_pool[l], pool_scale[l])
            h = h + out
            new_pool.append(st)
        else:
            j = l - N_A_LAYERS
            hn = _rmsnorm(h, norm_b[j])
            if k_buf is None:
                h = h + _attn_prompt(hn, k_sh, v_sh, w_q[j], w_o[j], sinks[j])
            else:
                h = h + _attn_sample(hn, k_sh, v_sh, k_buf, v_buf, pos0, w_q[j], w_o[j], sinks[j])
        h = h + _sq_relu_mlp(_rmsnorm(h, norm_mlp[l]), w_up[l], w_down[l])
        if l == N_A_LAYERS - 1:
            hk = _rmsnorm(h, norm_kv)
            k_sh = (hk @ w_k).reshape(B, T, N_KV_HEADS, HEAD_DIM)
            v_sh = (hk @ w_v).reshape(B, T, N_KV_HEADS, HEAD_DIM)
    y = _rmsnorm(h, norm_f)
    if k_buf is None:
        k_win, v_win = k_sh[:, T - wb:], v_sh[:, T - wb:]
    else:
        k_win = jnp.concatenate([k_buf.astype(k_sh.dtype), k_sh], axis=1)[:, -wb:]
        v_win = jnp.concatenate([v_buf.astype(v_sh.dtype), v_sh], axis=1)[:, -wb:]
    return y, jnp.stack(new_pool), k_win, v_win


def setup_inputs(seed: int = 0) -> dict:
    key = jax.random.key(seed)
    ks = jax.random.split(key, 20)
    f32 = jnp.float32
    wb = min(WINDOW, PAST_LEN)
    nrm = lambda k, s: jax.random.normal(k, s, f32)
    gain = lambda k, s: 1.0 + 0.02 * nrm(k, s)
    return {
        "x_prompt": nrm(ks[0], (BATCH, SEQ, D_MODEL)),
        "x_sample": nrm(ks[1], (DEC_BATCH, DEC_SEQ, D_MODEL)),
        "state_pool": nrm(ks[2], (N_A_LAYERS, DEC_BATCH, POOL_STATE, D_MODEL)),
        "cache_k_win": nrm(ks[3], (DEC_BATCH, wb, N_KV_HEADS, HEAD_DIM)),
        "cache_v_win": nrm(ks[4], (DEC_BATCH, wb, N_KV_HEADS, HEAD_DIM)),
        "norm_a": gain(ks[5], (N_A_LAYERS, D_MODEL)),
        "w_pool": nrm(ks[6], (N_A_LAYERS, N_POOL_GROUPS, POOL_GROUP, POOL_GROUP)) * POOL_GROUP ** -0.5,
        "pool_scale": gain(ks[7], (N_A_LAYERS, D_MODEL)),
        "norm_kv": gain(ks[8], (D_MODEL,)),
        "w_k": nrm(ks[9], (D_MODEL, N_KV_HEADS * HEAD_DIM)) * D_MODEL ** -0.5,
        "w_v": nrm(ks[10], (D_MODEL, N_KV_HEADS * HEAD_DIM)) * D_MODEL ** -0.5,
        "norm_b": gain(ks[11], (N_B_LAYERS, D_MODEL)),
        "w_q": nrm(ks[12], (N_B_LAYERS, D_MODEL, N_HEADS * HEAD_DIM)) * D_MODEL ** -0.5,
        "w_o": nrm(ks[13], (N_B_LAYERS, N_HEADS * HEAD_DIM, D_MODEL)) * (N_HEADS * HEAD_DIM) ** -0.5,
        "sinks": 0.5 * nrm(ks[14], (N_B_LAYERS, N_HEADS)),
        "norm_mlp": gain(ks[15], (DEPTH, D_MODEL)),
        "w_up": nrm(ks[16], (DEPTH, D_MODEL, D_FF)) * D_MODEL ** -0.5,
        "w_down": nrm(ks[17], (DEPTH, D_FF, D_MODEL)) * D_FF ** -0.5,
        "norm_f": gain(ks[18], (D_MODEL,)),
    }


def reference(x_prompt, x_sample, state_pool, cache_k_win, cache_v_win, norm_a, w_pool, pool_scale,
              norm_kv, w_k, w_v, norm_b, w_q, w_o, sinks, norm_mlp, w_up, w_down, norm_f):
    zero_pool = jnp.zeros((N_A_LAYERS, x_prompt.shape[0], POOL_STATE, D_MODEL), x_prompt.dtype)
    y_prompt, pool_p, k_p, v_p = _trunk(
        x_prompt, zero_pool, None, None, 0, norm_a, w_pool, pool_scale, norm_kv, w_k, w_v,
        norm_b, w_q, w_o, sinks, norm_mlp, w_up, w_down, norm_f)
    y_sample, pool_s, k_s, v_s = _trunk(
        x_sample, state_pool, cache_k_win, cache_v_win, PAST_LEN, norm_a, w_pool, pool_scale,
        norm_kv, w_k, w_v, norm_b, w_q, w_o, sinks, norm_mlp, w_up, w_down, norm_f)
    return (y_prompt, y_sample, pool_p, pool_s, k_p, v_p, k_s, v_s)
```

```python
import functools

import jax
import jax.numpy as jnp
from jax import lax
from jax.experimental import pallas as pl
from jax.experimental.pallas import tpu as pltpu

D_MODEL = 2048
BATCH = 4
SEQ = 2048
DEC_BATCH = 32
DEC_SEQ = 8
N_A_LAYERS = 2
N_B_LAYERS = 2
DEPTH = 4
POOL_WINDOWS = (2, 4, 8, 16)
POOL_GROUP = D_MODEL // len(POOL_WINDOWS)
POOL_STATE = 15
HEAD_DIM = 64
N_HEADS = 32
N_KV_HEADS = 4
KV_DIM = N_KV_HEADS * HEAD_DIM
GROUP_LANES = D_MODEL // N_KV_HEADS
WINDOW = 128
D_FF = 4 * D_MODEL
RMS_EPS = 1e-5

N_PROMPT = BATCH * SEQ
N_SAMPLE = DEC_BATCH * DEC_SEQ
N_ROWS = N_PROMPT + N_SAMPLE

ROW_TILE = 256
N_ROW_TILES = N_ROWS // ROW_TILE
TILES_PER_SEQ = SEQ // ROW_TILE
N_PROMPT_TILES = N_PROMPT // ROW_TILE
HALO = 16
MLP_ROW_TILE = 768
MLP_FF_TILE = 512
KEY_SLOTS = 2 * WINDOW

VMEM_LIMIT = 56 * 1024 * 1024

_BF16 = jnp.bfloat16
_F32 = jnp.float32


def _rms(x, g):
    ms = jnp.mean(x * x, axis=-1, keepdims=True)
    return (x * lax.rsqrt(ms + RMS_EPS)) * g


def _dot(a, b):
    return jnp.dot(a, b, preferred_element_type=_F32)


def _pool_kernel(h_ref, halo_ref, past_ref, g_ref, w_ref, sc_ref,
                 out_ref, poolp_ref, pools_ref, ext_ref, ext3_ref):
    i = pl.program_id(0)
    g = g_ref[...]

    @pl.when(i < N_PROMPT_TILES)
    def _prompt():
        x = h_ref[...]
        u = _rms(x, g)
        first = (i % TILES_PER_SEQ) == 0
        halo = jnp.where(first, 0.0, _rms(halo_ref[...], g))
        ext_ref[0:HALO, :] = halo
        ext_ref[HALO:HALO + ROW_TILE, :] = u
        poolp_ref[0] = u[ROW_TILE - HALO:, :]
        pos = (i % TILES_PER_SEQ) * ROW_TILE + lax.broadcasted_iota(jnp.int32, (ROW_TILE, 1), 0)
        for gi, w in enumerate(POOL_WINDOWS):
            sl = slice(gi * POOL_GROUP, (gi + 1) * POOL_GROUP)
            ug = u[:, sl]
            s = ug
            for j in range(1, w):
                s = s + ext_ref[HALO - j:HALO - j + ROW_TILE, sl]
            inv = 1.0 / jnp.minimum(pos + 1, w).astype(_F32)
            d = s * inv - ug
            o = _dot(d.astype(_BF16), w_ref[gi])
            out_ref[:, sl] = x[:, sl] + o * sc_ref[:, sl]

    @pl.when(i == N_PROMPT_TILES)
    def _sample():
        x = h_ref[...]
        u = _rms(x, g)
        u3 = u.reshape(DEC_BATCH, DEC_SEQ, D_MODEL)
        ext3_ref[:, 0:HALO, :] = past_ref[...]
        ext3_ref[:, HALO:HALO + DEC_SEQ, :] = u3
        pools_ref[:, 0:DEC_SEQ, :] = past_ref[:, HALO - DEC_SEQ:HALO, :]
        pools_ref[:, DEC_SEQ:HALO, :] = u3
        for gi, w in enumerate(POOL_WINDOWS):
            sl = slice(gi * POOL_GROUP, (gi + 1) * POOL_GROUP)
            ug = u3[:, :, sl]
            s = ug
            for j in range(1, w):
                s = s + ext3_ref[:, HALO - j:HALO - j + DEC_SEQ, sl]
            d = s * (1.0 / w) - ug
            o = _dot(d.reshape(N_SAMPLE, POOL_GROUP).astype(_BF16), w_ref[gi])
            out_ref[:, sl] = x[:, sl] + o * sc_ref[:, sl]


def _pool_layer(h, past16, gain, w_pool, scale):
    halo_blocks_per_tile = ROW_TILE // HALO
    return pl.pallas_call(
        _pool_kernel,
        grid=(N_ROW_TILES,),
        in_specs=[
            pl.BlockSpec((ROW_TILE, D_MODEL), lambda i: (i, 0)),
            pl.BlockSpec((HALO, D_MODEL), lambda i: (jnp.maximum(i * halo_blocks_per_tile - 1, 0), 0)),
            pl.BlockSpec((DEC_BATCH, HALO, D_MODEL), lambda i: (0, 0, 0), pipeline_mode=pl.Buffered(1)),
            pl.BlockSpec((1, D_MODEL), lambda i: (0, 0)),
            pl.BlockSpec((len(POOL_WINDOWS), POOL_GROUP, POOL_GROUP), lambda i: (0, 0, 0)),
            pl.BlockSpec((1, D_MODEL), lambda i: (0, 0)),
        ],
        out_specs=[
            pl.BlockSpec((ROW_TILE, D_MODEL), lambda i: (i, 0)),
            pl.BlockSpec((1, HALO, D_MODEL),
                         lambda i: (jnp.minimum(i // TILES_PER_SEQ, BATCH - 1), 0, 0)),
            pl.BlockSpec((DEC_BATCH, HALO, D_MODEL), lambda i: (0, 0, 0)),
        ],
        out_shape=[
            jax.ShapeDtypeStruct((N_ROWS, D_MODEL), _F32),
            jax.ShapeDtypeStruct((BATCH, HALO, D_MODEL), _F32),
            jax.ShapeDtypeStruct((DEC_BATCH, HALO, D_MODEL), _F32),
        ],
        scratch_shapes=[
            pltpu.VMEM((HALO + ROW_TILE, D_MODEL), _F32),
            pltpu.VMEM((DEC_BATCH, HALO + DEC_SEQ, D_MODEL), _F32),
        ],
        compiler_params=pltpu.CompilerParams(
            dimension_semantics=("arbitrary",), vmem_limit_bytes=VMEM_LIMIT),
        name="pool_layer",
    )(h, h, past16, gain, w_pool, scale)


def _mlp_kernel(h_ref, g_ref, wup_ref, wdn_ref, gf_ref, out_ref, hn_ref, *, final_norm):
    f = pl.program_id(1)

    @pl.when(f == 0)
    def _init():
        x = h_ref[...]
        hn_ref[...] = _rms(x, g_ref[...]).astype(_BF16)
        out_ref[...] = x

    a = jnp.maximum(_dot(hn_ref[...], wup_ref[...]), 0.0)
    out_ref[...] += _dot((a * a).astype(_BF16), wdn_ref[...])

    if final_norm:
        @pl.when(f == pl.num_programs(1) - 1)
        def _finish():
            out_ref[...] = _rms(out_ref[...], gf_ref[...])


def _mlp_layer(h, gain, w_up, w_down, gain_final, final_norm):
    return pl.pallas_call(
        functools.partial(_mlp_kernel, final_norm=final_norm),
        grid=(N_ROWS // MLP_ROW_TILE, D_FF // MLP_FF_TILE),
        in_specs=[
            pl.BlockSpec((MLP_ROW_TILE, D_MODEL), lambda i, f: (i, 0)),
            pl.BlockSpec((1, D_MODEL), lambda i, f: (0, 0)),
            pl.BlockSpec((D_MODEL, MLP_FF_TILE), lambda i, f: (0, f)),
            pl.BlockSpec((MLP_FF_TILE, D_MODEL), lambda i, f: (f, 0)),
            pl.BlockSpec((1, D_MODEL), lambda i, f: (0, 0)),
        ],
        out_specs=pl.BlockSpec((MLP_ROW_TILE, D_MODEL), lambda i, f: (i, 0)),
        out_shape=jax.ShapeDtypeStruct((N_ROWS, D_MODEL), _F32),
        scratch_shapes=[pltpu.VMEM((MLP_ROW_TILE, D_MODEL), _BF16)],
        compiler_params=pltpu.CompilerParams(
            dimension_semantics=("arbitrary", "arbitrary"), vmem_limit_bytes=VMEM_LIMIT),
        name="mlp_layer",
    )(h, gain, w_up, w_down, gain_final)


def _kv_kernel(h_ref, g_ref, w_ref, out_ref):
    hk = _rms(h_ref[...], g_ref[...]).astype(_BF16)
    out_ref[...] = _dot(hk, w_ref[...])


def _kv_proj(h, gain, w_kv):
    return pl.pallas_call(
        _kv_kernel,
        grid=(N_ROW_TILES,),
        in_specs=[
            pl.BlockSpec((ROW_TILE, D_MODEL), lambda i: (i, 0)),
            pl.BlockSpec((1, D_MODEL), lambda i: (0, 0)),
            pl.BlockSpec((D_MODEL, 2 * KV_DIM), lambda i: (0, 0)),
        ],
        out_specs=pl.BlockSpec((ROW_TILE, 2 * KV_DIM), lambda i: (i, 0)),
        out_shape=jax.ShapeDtypeStruct((N_ROWS, 2 * KV_DIM), _F32),
        compiler_params=pltpu.CompilerParams(
            dimension_semantics=("arbitrary",), vmem_limit_bytes=VMEM_LIMIT),
        name="kv_proj",
    )(h, gain, w_kv)


def _attn_core(q_groups, k_full, v_full, allowed, tq, sink_ref):
    kt = k_full.T.astype(_BF16)
    lane = lax.broadcasted_iota(jnp.int32, (KEY_SLOTS, 2 * HEAD_DIM), 1)
    low_half = lane < HEAD_DIM
    zk = jnp.zeros((HEAD_DIM, KEY_SLOTS), _BF16)
    outs = []
    for g in range(N_KV_HEADS):
        kgt = kt[g * HEAD_DIM:(g + 1) * HEAD_DIM, :]
        kk = jnp.concatenate([jnp.concatenate([kgt, zk], axis=1),
                              jnp.concatenate([zk, kgt], axis=1)], axis=0)
        c, e0 = divmod(g, 2)
        va = v_full[:, c * 2 * HEAD_DIM:(c + 1) * 2 * HEAD_DIM]
        vr = pltpu.roll(va, HEAD_DIM, 1)
        top = jnp.where(low_half, va if e0 == 0 else vr, 0.0)
        bot = jnp.where(low_half, 0.0, vr if e0 == 0 else va)
        vv = jnp.concatenate([top, bot], axis=0).astype(_BF16)
        qg = q_groups[g]
        qs = jnp.concatenate([qg[:, p * 128:(p + 1) * 128] for p in range(4)], axis=0)
        s = _dot(qs.astype(_BF16), kk)
        p_rows = []
        for p in range(4):
            p_cols = []
            for e in range(2):
                sink = sink_ref[g * 8 + p * 2 + e]
                spe = s[p * tq:(p + 1) * tq, e * KEY_SLOTS:(e + 1) * KEY_SLOTS]
                spe = jnp.where(allowed, spe, -jnp.inf)
                m = jnp.maximum(jnp.max(spe, axis=-1, keepdims=True), sink)
                pe = jnp.exp(spe - m)
                l = jnp.sum(pe, axis=-1, keepdims=True) + jnp.exp(sink - m)
                p_cols.append(pe * (1.0 / l))
            p_rows.append(jnp.concatenate(p_cols, axis=1))
        pm = jnp.concatenate(p_rows, axis=0).astype(_BF16)
        o = _dot(pm, vv)
        outs.append(jnp.concatenate([o[p * tq:(p + 1) * tq, :] for p in range(4)], axis=1))
    return outs


def _attn_kernel(sink_ref, h_ref, kv_ref, kvp_ref, ck_ref, cv_ref, g_ref, wq_ref, wo_ref,
                 out_ref, q_ref, o_ref):
    i = pl.program_id(0)
    x = h_ref[...]
    hn = _rms(x, g_ref[...]).astype(_BF16)
    q = _dot(hn, wq_ref[...]) * (HEAD_DIM ** -0.5)
    for g in range(N_KV_HEADS):
        q_ref[g] = q[:, g * GROUP_LANES:(g + 1) * GROUP_LANES]

    @pl.when(i < N_PROMPT_TILES)
    def _prompt():
        first = (i % TILES_PER_SEQ) == 0
        qi = lax.broadcasted_iota(jnp.int32, (WINDOW, KEY_SLOTS), 0)
        kj = lax.broadcasted_iota(jnp.int32, (WINDOW, KEY_SLOTS), 1)
        band = (kj > qi) & (kj <= qi + WINDOW)
        kv_cat = jnp.concatenate([kvp_ref[...], kv_ref[...]], axis=0)
        for n in range(ROW_TILE // WINDOW):
            allowed = band
            if n == 0:
                allowed = band & ((kj >= WINDOW) | jnp.logical_not(first))
            rows = slice(n * WINDOW, (n + 1) * WINDOW)
            kb = kv_cat[n * WINDOW:n * WINDOW + KEY_SLOTS, 0:KV_DIM]
            vb = kv_cat[n * WINDOW:n * WINDOW + KEY_SLOTS, KV_DIM:2 * KV_DIM]
            outs = _attn_core([q_ref[g, rows, :] for g in range(N_KV_HEADS)],
                              kb, vb, allowed, WINDOW, sink_ref)
            for g in range(N_KV_HEADS):
                o_ref[g, rows, :] = outs[g]

    @pl.when(i == N_PROMPT_TILES)
    def _sample():
        qi = lax.broadcasted_iota(jnp.int32, (DEC_SEQ, KEY_SLOTS), 0)
        kj = lax.broadcasted_iota(jnp.int32, (DEC_SEQ, KEY_SLOTS), 1)
        allowed = ((kj < WINDOW) & (kj > qi)) | ((kj >= WINDOW) & (kj - WINDOW <= qi))
        pad = jnp.zeros((KEY_SLOTS - WINDOW - DEC_SEQ, KV_DIM), _F32)

        def body(b, carry):
            rows = pl.ds(pl.multiple_of(b * DEC_SEQ, DEC_SEQ), DEC_SEQ)
            kb = jnp.concatenate([ck_ref[b], kv_ref[rows, 0:KV_DIM], pad], axis=0)
            vb = jnp.concatenate([cv_ref[b], kv_ref[rows, KV_DIM:2 * KV_DIM], pad], axis=0)
            outs = _attn_core([q_ref[g, rows, :] for g in range(N_KV_HEADS)],
                              kb, vb, allowed, DEC_SEQ, sink_ref)
            for g in range(N_KV_HEADS):
                o_ref[g, rows, :] = outs[g]
            return carry

        lax.fori_loop(0, DEC_BATCH, body, 0)

    acc = x
    for g in range(N_KV_HEADS):
        acc = acc + _dot(o_ref[g].astype(_BF16), wo_ref[g * GROUP_LANES:(g + 1) * GROUP_LANES, :])
    out_ref[...] = acc


def _attn_layer(h, kv, cache_k, cache_v, gain, w_q, w_o, sinks):
    kv_blocks_per_tile = ROW_TILE // WINDOW
    const1 = pl.Buffered(1)
    return pl.pallas_call(
        _attn_kernel,
        grid_spec=pltpu.PrefetchScalarGridSpec(
            num_scalar_prefetch=0,
            grid=(N_ROW_TILES,),
            in_specs=[
                pl.BlockSpec(memory_space=pltpu.SMEM),
                pl.BlockSpec((ROW_TILE, D_MODEL), lambda i: (i, 0)),
                pl.BlockSpec((ROW_TILE, 2 * KV_DIM), lambda i: (i, 0)),
                pl.BlockSpec((WINDOW, 2 * KV_DIM),
                             lambda i: (jnp.maximum(i * kv_blocks_per_tile - 1, 0), 0)),
                pl.BlockSpec((DEC_BATCH, WINDOW, KV_DIM), lambda i: (0, 0, 0), pipeline_mode=const1),
                pl.BlockSpec((DEC_BATCH, WINDOW, KV_DIM), lambda i: (0, 0, 0), pipeline_mode=const1),
                pl.BlockSpec((1, D_MODEL), lambda i: (0, 0)),
                pl.BlockSpec((D_MODEL, D_MODEL), lambda i: (0, 0), pipeline_mode=const1),
                pl.BlockSpec((D_MODEL, D_MODEL), lambda i: (0, 0), pipeline_mode=const1),
            ],
            out_specs=pl.BlockSpec((ROW_TILE, D_MODEL), lambda i: (i, 0)),
            scratch_shapes=[
                pltpu.VMEM((N_KV_HEADS, ROW_TILE, GROUP_LANES), _F32),
                pltpu.VMEM((N_KV_HEADS, ROW_TILE, GROUP_LANES), _F32),
            ],
        ),
        out_shape=jax.ShapeDtypeStruct((N_ROWS, D_MODEL), _F32),
        compiler_params=pltpu.CompilerParams(
            dimension_semantics=("arbitrary",), vmem_limit_bytes=VMEM_LIMIT),
        name="attn_layer",
    )(sinks, h, kv, kv, cache_k, cache_v, gain, w_q, w_o)


def kernel(x_prompt, x_sample, state_pool, cache_k_win, cache_v_win, norm_a, w_pool, pool_scale,
           norm_kv, w_k, w_v, norm_b, w_q, w_o, sinks, norm_mlp, w_up, w_down, norm_f):
    h = jnp.concatenate([x_prompt.reshape(N_PROMPT, D_MODEL),
                         x_sample.reshape(N_SAMPLE, D_MODEL)], axis=0)
    past16 = jnp.pad(state_pool, ((0, 0), (0, 0), (HALO - POOL_STATE, 0), (0, 0)))
    cache_k = cache_k_win.reshape(DEC_BATCH, WINDOW, KV_DIM)
    cache_v = cache_v_win.reshape(DEC_BATCH, WINDOW, KV_DIM)
    w_kv = jnp.concatenate([w_k, w_v], axis=1).astype(_BF16)
    row = lambda v: v.reshape(1, D_MODEL)

    pool_p, pool_s = [], []
    kv = None
    for l in range(DEPTH):
        if l < N_A_LAYERS:
            h, pp, ps = _pool_layer(h, past16[l], row(norm_a[l]), w_pool[l].astype(_BF16),
                                    row(pool_scale[l]))
            pool_p.append(pp[:, HALO - POOL_STATE:])
            pool_s.append(ps[:, HALO - POOL_STATE:])
        else:
            j = l - N_A_LAYERS
            h = _attn_layer(h, kv, cache_k, cache_v, row(norm_b[j]), w_q[j].astype(_BF16),
                            w_o[j].astype(_BF16), sinks[j])
        h = _mlp_layer(h, row(norm_mlp[l]), w_up[l].astype(_BF16), w_down[l].astype(_BF16),
                       row(norm_f), final_norm=(l == DEPTH - 1))
        if l == N_A_LAYERS - 1:
            kv = _kv_proj(h, row(norm_kv), w_kv)

    y_prompt = h[:N_PROMPT].reshape(BATCH, SEQ, D_MODEL)
    y_sample = h[N_PROMPT:].reshape(DEC_BATCH, DEC_SEQ, D_MODEL)
    k_all, v_all = kv[:, :KV_DIM], kv[:, KV_DIM:]

    def prompt_window(a):
        return a[:N_PROMPT].reshape(BATCH, SEQ, N_KV_HEADS, HEAD_DIM)[:, SEQ - WINDOW:]

    def sample_window(buf, a):
        new = a[N_PROMPT:].reshape(DEC_BATCH, DEC_SEQ, N_KV_HEADS, HEAD_DIM)
        return jnp.concatenate([buf, new], axis=1)[:, DEC_SEQ:]

    return (y_prompt, y_sample, jnp.stack(pool_p), jnp.stack(pool_s),
            prompt_window(k_all), prompt_window(v_all),
            sample_window(cache_k_win, k_all), sample_window(cache_v_win, v_all))
```

```python
import functools

import jax
import jax.numpy as jnp
from jax import lax
from jax.experimental import pallas as pl
from jax.experimental.pallas import tpu as pltpu

D_MODEL = 2048
BATCH = 4
SEQ = 2048
DEC_BATCH = 32
DEC_SEQ = 8
N_A_LAYERS = 2
N_B_LAYERS = 2
DEPTH = 4
POOL_WINDOWS = (2, 4, 8, 16)
POOL_GROUP = D_MODEL // len(POOL_WINDOWS)
POOL_STATE = 15
HEAD_DIM = 64
N_HEADS = 32
N_KV_HEADS = 4
KV_DIM = N_KV_HEADS * HEAD_DIM
GROUP_LANES = D_MODEL // N_KV_HEADS
WINDOW = 128
D_FF = 4 * D_MODEL
RMS_EPS = 1e-5

N_PROMPT = BATCH * SEQ
N_SAMPLE = DEC_BATCH * DEC_SEQ
N_ROWS = N_PROMPT + N_SAMPLE

ROW_TILE = 256
N_ROW_TILES = N_ROWS // ROW_TILE
TILES_PER_SEQ = SEQ // ROW_TILE
N_PROMPT_TILES = N_PROMPT // ROW_TILE
HALO = 16
MLP_ROW_TILE = 768
MLP_FF_TILE = 1024
N_MLP_TILES = N_ROWS // MLP_ROW_TILE
MLP_SAMPLE_ROW0 = N_PROMPT - (N_MLP_TILES - 1) * MLP_ROW_TILE
assert N_MLP_TILES * MLP_ROW_TILE == N_ROWS and MLP_SAMPLE_ROW0 + N_SAMPLE == MLP_ROW_TILE
KEY_SLOTS = 2 * WINDOW

VMEM_LIMIT = 56 * 1024 * 1024

_BF16 = jnp.bfloat16
_F32 = jnp.float32


def _rms(x, g):
    ms = jnp.mean(x * x, axis=-1, keepdims=True)
    return (x * lax.rsqrt(ms + RMS_EPS)) * g


def _dot(a, b):
    return jnp.dot(a, b, preferred_element_type=_F32)


def _pool_kernel(hp_ref, halo_ref, hs_ref, past_ref, g_ref, w_ref, sc_ref,
                 out_ref, poolp_ref, pools_ref, ext_ref, ext3_ref):
    i = pl.program_id(0)
    g = g_ref[...]

    @pl.when(i < N_PROMPT_TILES)
    def _prompt():
        x = hp_ref[...]
        u = _rms(x, g)
        first = (i % TILES_PER_SEQ) == 0
        halo = jnp.where(first, 0.0, _rms(halo_ref[...], g))
        ext_ref[0:HALO, :] = halo
        ext_ref[HALO:HALO + ROW_TILE, :] = u
        poolp_ref[0] = ext_ref[HALO + ROW_TILE - POOL_STATE:HALO + ROW_TILE, :]
        pos = (i % TILES_PER_SEQ) * ROW_TILE + lax.broadcasted_iota(jnp.int32, (ROW_TILE, 1), 0)
        for gi, w in enumerate(POOL_WINDOWS):
            sl = slice(gi * POOL_GROUP, (gi + 1) * POOL_GROUP)
            ug = u[:, sl]
            s = ug
            for j in range(1, w):
                s = s + ext_ref[HALO - j:HALO - j + ROW_TILE, sl]
            inv = 1.0 / jnp.minimum(pos + 1, w).astype(_F32)
            d = s * inv - ug
            o = _dot(d.astype(_BF16), w_ref[gi])
            out_ref[:, sl] = x[:, sl] + o * sc_ref[:, sl]

    @pl.when(i == N_PROMPT_TILES)
    def _sample():
        x = hs_ref[...]
        u = _rms(x, g)
        u3 = u.reshape(DEC_BATCH, DEC_SEQ, D_MODEL)
        ext3_ref[:, HALO - POOL_STATE:HALO, :] = past_ref[...]
        ext3_ref[:, HALO:HALO + DEC_SEQ, :] = u3
        pools_ref[:, 0:POOL_STATE - DEC_SEQ, :] = past_ref[:, DEC_SEQ:POOL_STATE, :]
        pools_ref[:, POOL_STATE - DEC_SEQ:POOL_STATE, :] = u3
        for gi, w in enumerate(POOL_WINDOWS):
            sl = slice(gi * POOL_GROUP, (gi + 1) * POOL_GROUP)
            ug = u3[:, :, sl]
            s = ug
            for j in range(1, w):
                s = s + ext3_ref[:, HALO - j:HALO - j + DEC_SEQ, sl]
            d = s * (1.0 / w) - ug
            o = _dot(d.reshape(N_SAMPLE, POOL_GROUP).astype(_BF16), w_ref[gi])
            out_ref[:, sl] = x[:, sl] + o * sc_ref[:, sl]


def _pool_layer(layer, h_prompt, h_sample, sample_tile, state_pool, norm_a, w_pool, pool_scale):
    halo_blocks_per_tile = ROW_TILE // HALO
    return pl.pallas_call(
        _pool_kernel,
        grid=(N_ROW_TILES,),
        in_specs=[
            pl.BlockSpec((ROW_TILE, D_MODEL), lambda i: (jnp.minimum(i, N_PROMPT_TILES - 1), 0)),
            pl.BlockSpec((HALO, D_MODEL),
                         lambda i: (jnp.clip(i * halo_blocks_per_tile - 1, 0,
                                             N_PROMPT // HALO - 1), 0)),
            pl.BlockSpec((ROW_TILE, D_MODEL), lambda i: (sample_tile, 0)),
            pl.BlockSpec((None, DEC_BATCH, POOL_STATE, D_MODEL), lambda i: (layer, 0, 0, 0),
                         pipeline_mode=pl.Buffered(1)),
            pl.BlockSpec((None, 1, D_MODEL), lambda i: (layer, 0, 0)),
            pl.BlockSpec((None, len(POOL_WINDOWS), POOL_GROUP, POOL_GROUP),
                         lambda i: (layer, 0, 0, 0)),
            pl.BlockSpec((None, 1, D_MODEL), lambda i: (layer, 0, 0)),
        ],
        out_specs=[
            pl.BlockSpec((ROW_TILE, D_MODEL), lambda i: (i, 0)),
            pl.BlockSpec((1, POOL_STATE, D_MODEL),
                         lambda i: (jnp.minimum(i // TILES_PER_SEQ, BATCH - 1), 0, 0)),
            pl.BlockSpec((DEC_BATCH, POOL_STATE, D_MODEL), lambda i: (0, 0, 0)),
        ],
        out_shape=[
            jax.ShapeDtypeStruct((N_ROWS, D_MODEL), _F32),
            jax.ShapeDtypeStruct((BATCH, POOL_STATE, D_MODEL), _F32),
            jax.ShapeDtypeStruct((DEC_BATCH, POOL_STATE, D_MODEL), _F32),
        ],
        scratch_shapes=[
            pltpu.VMEM((HALO + ROW_TILE, D_MODEL), _F32),
            pltpu.VMEM((DEC_BATCH, HALO + DEC_SEQ, D_MODEL), _F32),
        ],
        compiler_params=pltpu.CompilerParams(
            dimension_semantics=("arbitrary",), vmem_limit_bytes=VMEM_LIMIT),
        name="pool_layer",
    )(h_prompt, h_prompt, h_sample, state_pool, norm_a, w_pool, pool_scale)


def _mlp_kernel(h_ref, g_ref, wup_ref, wdn_ref, *rest, final):
    if final:
        gf_ref, out_ref, outs_ref, hn_ref = rest
    else:
        out_ref, hn_ref = rest
    i = pl.program_id(0)
    f = pl.program_id(1)

    @pl.when(f == 0)
    def _init():
        x = h_ref[...]
        hn_ref[...] = _rms(x, g_ref[...]).astype(_BF16)
        out_ref[...] = x

    a = jnp.maximum(_dot(hn_ref[...], wup_ref[...]), 0.0)
    out_ref[...] += _dot((a * a).astype(_BF16), wdn_ref[...])

    if final:
        last_f = f == pl.num_programs(1) - 1

        @pl.when(last_f)
        def _finish():
            out_ref[...] = _rms(out_ref[...], gf_ref[...])

        @pl.when(last_f & (i == N_MLP_TILES - 1))
        def _split():
            outs_ref[...] = out_ref[MLP_SAMPLE_ROW0:, :]


def _mlp_layer(layer, h, norm_mlp, w_up, w_down, norm_f=None):
    final = norm_f is not None
    in_specs = [
        pl.BlockSpec((MLP_ROW_TILE, D_MODEL), lambda i, f: (i, 0)),
        pl.BlockSpec((None, 1, D_MODEL), lambda i, f: (layer, 0, 0)),
        pl.BlockSpec((None, D_MODEL, MLP_FF_TILE), lambda i, f: (layer, 0, f)),
        pl.BlockSpec((None, MLP_FF_TILE, D_MODEL), lambda i, f: (layer, f, 0)),
    ]
    args = [h, norm_mlp, w_up, w_down]
    row_spec = pl.BlockSpec((MLP_ROW_TILE, D_MODEL), lambda i, f: (i, 0))
    if final:
        in_specs.append(pl.BlockSpec((1, D_MODEL), lambda i, f: (0, 0)))
        args.append(norm_f)
        out_specs = [row_spec, pl.BlockSpec((N_SAMPLE, D_MODEL), lambda i, f: (0, 0))]
        out_shape = [jax.ShapeDtypeStruct((N_PROMPT, D_MODEL), _F32),
                     jax.ShapeDtypeStruct((N_SAMPLE, D_MODEL), _F32)]
    else:
        out_specs = row_spec
        out_shape = jax.ShapeDtypeStruct((N_ROWS, D_MODEL), _F32)
    return pl.pallas_call(
        functools.partial(_mlp_kernel, final=final),
        grid=(N_MLP_TILES, D_FF // MLP_FF_TILE),
        in_specs=in_specs,
        out_specs=out_specs,
        out_shape=out_shape,
        scratch_shapes=[pltpu.VMEM((MLP_ROW_TILE, D_MODEL), _BF16)],
        compiler_params=pltpu.CompilerParams(
            dimension_semantics=("arbitrary", "arbitrary"), vmem_limit_bytes=VMEM_LIMIT),
        name="mlp_layer",
    )(*args)


def _kv_kernel(h_ref, g_ref, w_ref, out_ref):
    hk = _rms(h_ref[...], g_ref[...]).astype(_BF16)
    out_ref[...] = _dot(hk, w_ref[...])


def _kv_proj(h, gain, w_kv):
    return pl.pallas_call(
        _kv_kernel,
        grid=(N_ROW_TILES,),
        in_specs=[
            pl.BlockSpec((ROW_TILE, D_MODEL), lambda i: (i, 0)),
            pl.BlockSpec((1, D_MODEL), lambda i: (0, 0)),
            pl.BlockSpec((D_MODEL, 2 * KV_DIM), lambda i: (0, 0)),
        ],
        out_specs=pl.BlockSpec((ROW_TILE, 2 * KV_DIM), lambda i: (i, 0)),
        out_shape=jax.ShapeDtypeStruct((N_ROWS, 2 * KV_DIM), _F32),
        compiler_params=pltpu.CompilerParams(
            dimension_semantics=("arbitrary",), vmem_limit_bytes=VMEM_LIMIT),
        name="kv_proj",
    )(h, gain, w_kv)


def _attn_core(q_groups, k_full, v_full, allowed, tq, sink_ref):
    kt = k_full.T.astype(_BF16)
    lane = lax.broadcasted_iota(jnp.int32, (KEY_SLOTS, 2 * HEAD_DIM), 1)
    low_half = lane < HEAD_DIM
    zk = jnp.zeros((HEAD_DIM, KEY_SLOTS), _BF16)
    outs = []
    for g in range(N_KV_HEADS):
        kgt = kt[g * HEAD_DIM:(g + 1) * HEAD_DIM, :]
        kk = jnp.concatenate([jnp.concatenate([kgt, zk], axis=1),
                              jnp.concatenate([zk, kgt], axis=1)], axis=0)
        c, e0 = divmod(g, 2)
        va = v_full[:, c * 2 * HEAD_DIM:(c + 1) * 2 * HEAD_DIM]
        vr = pltpu.roll(va, HEAD_DIM, 1)
        top = jnp.where(low_half, va if e0 == 0 else vr, 0.0)
        bot = jnp.where(low_half, 0.0, vr if e0 == 0 else va)
        vv = jnp.concatenate([top, bot], axis=0).astype(_BF16)
        qg = q_groups[g]
        qs = jnp.concatenate([qg[:, p * 128:(p + 1) * 128] for p in range(4)], axis=0)
        s = _dot(qs.astype(_BF16), kk)
        p_rows = []
        for p in range(4):
            p_cols = []
            for e in range(2):
                sink = sink_ref[g * 8 + p * 2 + e]
                spe = s[p * tq:(p + 1) * tq, e * KEY_SLOTS:(e + 1) * KEY_SLOTS]
                spe = jnp.where(allowed, spe, -jnp.inf)
                m = jnp.maximum(jnp.max(spe, axis=-1, keepdims=True), sink)
                pe = jnp.exp(spe - m)
                l = jnp.sum(pe, axis=-1, keepdims=True) + jnp.exp(sink - m)
                p_cols.append(pe * (1.0 / l))
            p_rows.append(jnp.concatenate(p_cols, axis=1))
        pm = jnp.concatenate(p_rows, axis=0).astype(_BF16)
        o = _dot(pm, vv)
        outs.append(jnp.concatenate([o[p * tq:(p + 1) * tq, :] for p in range(4)], axis=1))
    return outs


def _attn_kernel(sink_ref, h_ref, kv_ref, kvp_ref, ck_ref, cv_ref, g_ref, wq_ref, wo_ref,
                 out_ref, q_ref, o_ref):
    i = pl.program_id(0)
    x = h_ref[...]
    hn = _rms(x, g_ref[...]).astype(_BF16)
    q = _dot(hn, wq_ref[...]) * (HEAD_DIM ** -0.5)
    for g in range(N_KV_HEADS):
        q_ref[g] = q[:, g * GROUP_LANES:(g + 1) * GROUP_LANES]

    @pl.when(i < N_PROMPT_TILES)
    def _prompt():
        first = (i % TILES_PER_SEQ) == 0
        qi = lax.broadcasted_iota(jnp.int32, (WINDOW, KEY_SLOTS), 0)
        kj = lax.broadcasted_iota(jnp.int32, (WINDOW, KEY_SLOTS), 1)
        band = (kj > qi) & (kj <= qi + WINDOW)
        kv_cat = jnp.concatenate([kvp_ref[...], kv_ref[...]], axis=0)
        for n in range(ROW_TILE // WINDOW):
            allowed = band
            if n == 0:
                allowed = band & ((kj >= WINDOW) | jnp.logical_not(first))
            rows = slice(n * WINDOW, (n + 1) * WINDOW)
            kb = kv_cat[n * WINDOW:n * WINDOW + KEY_SLOTS, 0:KV_DIM]
            vb = kv_cat[n * WINDOW:n * WINDOW + KEY_SLOTS, KV_DIM:2 * KV_DIM]
            outs = _attn_core([q_ref[g, rows, :] for g in range(N_KV_HEADS)],
                              kb, vb, allowed, WINDOW, sink_ref)
            for g in range(N_KV_HEADS):
                o_ref[g, rows, :] = outs[g]

    @pl.when(i == N_PROMPT_TILES)
    def _sample():
        qi = lax.broadcasted_iota(jnp.int32, (DEC_SEQ, KEY_SLOTS), 0)
        kj = lax.broadcasted_iota(jnp.int32, (DEC_SEQ, KEY_SLOTS), 1)
        allowed = ((kj < WINDOW) & (kj > qi)) | ((kj >= WINDOW) & (kj - WINDOW <= qi))
        pad = jnp.zeros((KEY_SLOTS - WINDOW - DEC_SEQ, KV_DIM), _F32)

        def body(b, carry):
            rows = pl.ds(pl.multiple_of(b * DEC_SEQ, DEC_SEQ), DEC_SEQ)
            kb = jnp.concatenate([ck_ref[b], kv_ref[rows, 0:KV_DIM], pad], axis=0)
            vb = jnp.concatenate([cv_ref[b], kv_ref[rows, KV_DIM:2 * KV_DIM], pad], axis=0)
            outs = _attn_core([q_ref[g, rows, :] for g in range(N_KV_HEADS)],
                              kb, vb, allowed, DEC_SEQ, sink_ref)
            for g in range(N_KV_HEADS):
                o_ref[g, rows, :] = outs[g]
            return carry

        lax.fori_loop(0, DEC_BATCH, body, 0)

    acc = x
    for g in range(N_KV_HEADS):
        acc = acc + _dot(o_ref[g].astype(_BF16), wo_ref[g * GROUP_LANES:(g + 1) * GROUP_LANES, :])
    out_ref[...] = acc


def _attn_layer(layer, h, kv, cache_k, cache_v, norm_b, w_q, w_o, sinks):
    kv_blocks_per_tile = ROW_TILE // WINDOW
    const1 = pl.Buffered(1)
    return pl.pallas_call(
        _attn_kernel,
        grid_spec=pltpu.PrefetchScalarGridSpec(
            num_scalar_prefetch=0,
            grid=(N_ROW_TILES,),
            in_specs=[
                pl.BlockSpec(memory_space=pltpu.SMEM),
                pl.BlockSpec((ROW_TILE, D_MODEL), lambda i: (i, 0)),
                pl.BlockSpec((ROW_TILE, 2 * KV_DIM), lambda i: (i, 0)),
                pl.BlockSpec((WINDOW, 2 * KV_DIM),
                             lambda i: (jnp.maximum(i * kv_blocks_per_tile - 1, 0), 0)),
                pl.BlockSpec((DEC_BATCH, WINDOW, KV_DIM), lambda i: (0, 0, 0), pipeline_mode=const1),
                pl.BlockSpec((DEC_BATCH, WINDOW, KV_DIM), lambda i: (0, 0, 0), pipeline_mode=const1),
                pl.BlockSpec((None, 1, D_MODEL), lambda i: (layer, 0, 0)),
                pl.BlockSpec((None, D_MODEL, D_MODEL), lambda i: (layer, 0, 0), pipeline_mode=const1),
                pl.BlockSpec((None, D_MODEL, D_MODEL), lambda i: (layer, 0, 0), pipeline_mode=const1),
            ],
            out_specs=pl.BlockSpec((ROW_TILE, D_MODEL), lambda i: (i, 0)),
            scratch_shapes=[
                pltpu.VMEM((N_KV_HEADS, ROW_TILE, GROUP_LANES), _F32),
                pltpu.VMEM((N_KV_HEADS, ROW_TILE, GROUP_LANES), _F32),
            ],
        ),
        out_shape=jax.ShapeDtypeStruct((N_ROWS, D_MODEL), _F32),
        compiler_params=pltpu.CompilerParams(
            dimension_semantics=("arbitrary",), vmem_limit_bytes=VMEM_LIMIT),
        name="attn_layer",
    )(sinks, h, kv, kv, cache_k, cache_v, norm_b, w_q, w_o)


def kernel(x_prompt, x_sample, state_pool, cache_k_win, cache_v_win, norm_a, w_pool, pool_scale,
           norm_kv, w_k, w_v, norm_b, w_q, w_o, sinks, norm_mlp, w_up, w_down, norm_f):
    cache_k = cache_k_win.reshape(DEC_BATCH, WINDOW, KV_DIM)
    cache_v = cache_v_win.reshape(DEC_BATCH, WINDOW, KV_DIM)
    rows3 = lambda v: v.reshape(v.shape[0], 1, D_MODEL)
    norm_a3, scale3, norm_b3, norm_mlp3 = rows3(norm_a), rows3(pool_scale), rows3(norm_b), rows3(norm_mlp)
    w_pool_b, w_q_b, w_o_b = w_pool.astype(_BF16), w_q.astype(_BF16), w_o.astype(_BF16)
    w_up_b, w_down_b = w_up.astype(_BF16), w_down.astype(_BF16)
    w_kv = jnp.concatenate([w_k, w_v], axis=1).astype(_BF16)

    pool_p, pool_s = [], []
    h_prompt = x_prompt.reshape(N_PROMPT, D_MODEL)
    h_sample, sample_tile = x_sample.reshape(N_SAMPLE, D_MODEL), 0
    for l in range(N_A_LAYERS):
        h, pp, ps = _pool_layer(l, h_prompt, h_sample, sample_tile, state_pool, norm_a3, w_pool_b, scale3)
        pool_p.append(pp)
        pool_s.append(ps)
        h = _mlp_layer(l, h, norm_mlp3, w_up_b, w_down_b)
        h_prompt, h_sample, sample_tile = h, h, N_PROMPT_TILES
    kv = _kv_proj(h, norm_kv.reshape(1, D_MODEL), w_kv)
    for j in range(N_B_LAYERS):
        l = N_A_LAYERS + j
        h = _attn_layer(j, h, kv, cache_k, cache_v, norm_b3, w_q_b, w_o_b, sinks[j])
        if l < DEPTH - 1:
            h = _mlp_layer(l, h, norm_mlp3, w_up_b, w_down_b)
        else:
            y_prompt, y_sample = _mlp_layer(l, h, norm_mlp3, w_up_b, w_down_b,
                                            norm_f=norm_f.reshape(1, D_MODEL))

    k_all, v_all = kv[:, :KV_DIM], kv[:, KV_DIM:]

    def prompt_window(a):
        return a[:N_PROMPT].reshape(BATCH, SEQ, N_KV_HEADS, HEAD_DIM)[:, SEQ - WINDOW:]

    def sample_window(buf, a):
        new = a[N_PROMPT:].reshape(DEC_BATCH, DEC_SEQ, N_KV_HEADS, HEAD_DIM)
        return jnp.concatenate([buf, new], axis=1)[:, DEC_SEQ:]

    return (y_prompt.reshape(BATCH, SEQ, D_MODEL), y_sample.reshape(DEC_BATCH, DEC_SEQ, D_MODEL),
            jnp.stack(pool_p), jnp.stack(pool_s),
            prompt_window(k_all), prompt_window(v_all),
            sample_window(cache_k_win, k_all), sample_window(cache_v_win, v_all))
```

```python
import functools

import jax
import jax.numpy as jnp
from jax import lax
from jax.experimental import pallas as pl
from jax.experimental.pallas import tpu as pltpu

D_MODEL = 2048
BATCH = 4
SEQ = 2048
DEC_BATCH = 32
DEC_SEQ = 8
N_A_LAYERS = 2
N_B_LAYERS = 2
DEPTH = 4
POOL_WINDOWS = (2, 4, 8, 16)
POOL_GROUP = D_MODEL // len(POOL_WINDOWS)
POOL_STATE = 15
HEAD_DIM = 64
N_HEADS = 32
N_KV_HEADS = 4
KV_DIM = N_KV_HEADS * HEAD_DIM
GROUP_LANES = D_MODEL // N_KV_HEADS
WINDOW = 128
D_FF = 4 * D_MODEL
RMS_EPS = 1e-5

N_PROMPT = BATCH * SEQ
N_SAMPLE = DEC_BATCH * DEC_SEQ
N_ROWS = N_PROMPT + N_SAMPLE

ROW_TILE = 256
N_ROW_TILES = N_ROWS // ROW_TILE
TILES_PER_SEQ = SEQ // ROW_TILE
N_PROMPT_TILES = N_PROMPT // ROW_TILE
HALO = 16
MLP_ROW_TILE = 768
MLP_FF_TILE = 1024
N_MLP_TILES = N_ROWS // MLP_ROW_TILE
MLP_SAMPLE_ROW0 = N_PROMPT - (N_MLP_TILES - 1) * MLP_ROW_TILE
assert N_MLP_TILES * MLP_ROW_TILE == N_ROWS and MLP_SAMPLE_ROW0 + N_SAMPLE == MLP_ROW_TILE
KEY_SLOTS = 2 * WINDOW
CONVERT_STEPS = 64

VMEM_LIMIT = 56 * 1024 * 1024

_BF16 = jnp.bfloat16
_F32 = jnp.float32


def _rms(x, g):
    ms = jnp.mean(x * x, axis=-1, keepdims=True)
    return (x * lax.rsqrt(ms + RMS_EPS)) * g


def _dot(a, b):
    return jnp.dot(a, b, preferred_element_type=_F32)


def _pool_kernel(hp_ref, halo_ref, hs_ref, past_ref, g_ref, w_ref, sc_ref,
                 out_ref, poolp_ref, pools_ref, ext_ref, ext3_ref):
    i = pl.program_id(0)
    g = g_ref[...]

    @pl.when(i < N_PROMPT_TILES)
    def _prompt():
        x = hp_ref[...]
        u = _rms(x, g)
        first = (i % TILES_PER_SEQ) == 0
        halo = jnp.where(first, 0.0, _rms(halo_ref[...], g))
        ext_ref[0:HALO, :] = halo
        ext_ref[HALO:HALO + ROW_TILE, :] = u
        poolp_ref[0] = ext_ref[HALO + ROW_TILE - POOL_STATE:HALO + ROW_TILE, :]
        pos = (i % TILES_PER_SEQ) * ROW_TILE + lax.broadcasted_iota(jnp.int32, (ROW_TILE, 1), 0)
        for gi, w in enumerate(POOL_WINDOWS):
            sl = slice(gi * POOL_GROUP, (gi + 1) * POOL_GROUP)
            ug = u[:, sl]
            s = ug
            for j in range(1, w):
                s = s + ext_ref[HALO - j:HALO - j + ROW_TILE, sl]
            inv = 1.0 / jnp.minimum(pos + 1, w).astype(_F32)
            d = s * inv - ug
            o = _dot(d.astype(_BF16), w_ref[gi])
            out_ref[:, sl] = x[:, sl] + o * sc_ref[:, sl]

    @pl.when(i == N_PROMPT_TILES)
    def _sample():
        x = hs_ref[...]
        u = _rms(x, g)
        u3 = u.reshape(DEC_BATCH, DEC_SEQ, D_MODEL)
        ext3_ref[:, HALO - POOL_STATE:HALO, :] = past_ref[...]
        ext3_ref[:, HALO:HALO + DEC_SEQ, :] = u3
        pools_ref[:, 0:POOL_STATE - DEC_SEQ, :] = past_ref[:, DEC_SEQ:POOL_STATE, :]
        pools_ref[:, POOL_STATE - DEC_SEQ:POOL_STATE, :] = u3
        for gi, w in enumerate(POOL_WINDOWS):
            sl = slice(gi * POOL_GROUP, (gi + 1) * POOL_GROUP)
            ug = u3[:, :, sl]
            s = ug
            for j in range(1, w):
                s = s + ext3_ref[:, HALO - j:HALO - j + DEC_SEQ, sl]
            d = s * (1.0 / w) - ug
            o = _dot(d.reshape(N_SAMPLE, POOL_GROUP).astype(_BF16), w_ref[gi])
            out_ref[:, sl] = x[:, sl] + o * sc_ref[:, sl]


def _pool_layer(layer, h_prompt, h_sample, sample_tile, state_pool, norm_a, w_pool, pool_scale):
    halo_blocks_per_tile = ROW_TILE // HALO
    return pl.pallas_call(
        _pool_kernel,
        grid=(N_ROW_TILES,),
        in_specs=[
            pl.BlockSpec((ROW_TILE, D_MODEL), lambda i: (jnp.minimum(i, N_PROMPT_TILES - 1), 0)),
            pl.BlockSpec((HALO, D_MODEL),
                         lambda i: (jnp.clip(i * halo_blocks_per_tile - 1, 0,
                                             N_PROMPT // HALO - 1), 0)),
            pl.BlockSpec((ROW_TILE, D_MODEL), lambda i: (sample_tile, 0)),
            pl.BlockSpec((None, DEC_BATCH, POOL_STATE, D_MODEL), lambda i: (layer, 0, 0, 0),
                         pipeline_mode=pl.Buffered(1)),
            pl.BlockSpec((None, 1, D_MODEL), lambda i: (layer, 0, 0)),
            pl.BlockSpec((None, len(POOL_WINDOWS), POOL_GROUP, POOL_GROUP),
                         lambda i: (layer, 0, 0, 0)),
            pl.BlockSpec((None, 1, D_MODEL), lambda i: (layer, 0, 0)),
        ],
        out_specs=[
            pl.BlockSpec((ROW_TILE, D_MODEL), lambda i: (i, 0)),
            pl.BlockSpec((1, POOL_STATE, D_MODEL),
                         lambda i: (jnp.minimum(i // TILES_PER_SEQ, BATCH - 1), 0, 0)),
            pl.BlockSpec((DEC_BATCH, POOL_STATE, D_MODEL), lambda i: (0, 0, 0)),
        ],
        out_shape=[
            jax.ShapeDtypeStruct((N_ROWS, D_MODEL), _F32),
            jax.ShapeDtypeStruct((BATCH, POOL_STATE, D_MODEL), _F32),
            jax.ShapeDtypeStruct((DEC_BATCH, POOL_STATE, D_MODEL), _F32),
        ],
        scratch_shapes=[
            pltpu.VMEM((HALO + ROW_TILE, D_MODEL), _F32),
            pltpu.VMEM((DEC_BATCH, HALO + DEC_SEQ, D_MODEL), _F32),
        ],
        compiler_params=pltpu.CompilerParams(
            dimension_semantics=("arbitrary",), vmem_limit_bytes=VMEM_LIMIT),
        name="pool_layer",
    )(h_prompt, h_prompt, h_sample, state_pool, norm_a, w_pool, pool_scale)


def _mlp_kernel(*refs, final, n_convert):
    h_ref, g_ref, wup_ref, wdn_ref = refs[:4]
    n_in = 4 + int(final) + n_convert
    gf_ref = refs[4] if final else None
    convert_src = refs[n_in - n_convert:n_in]
    out_ref = refs[n_in]
    outs_ref = refs[n_in + 1] if final else None
    n_out = 1 + int(final)
    convert_dst = refs[n_in + n_out:n_in + n_out + n_convert]
    hn_ref = refs[n_in + n_out + n_convert]
    i = pl.program_id(0)
    f = pl.program_id(1)

    @pl.when(f == 0)
    def _init():
        x = h_ref[...]
        hn_ref[...] = _rms(x, g_ref[...]).astype(_BF16)
        out_ref[...] = x

    a = jnp.maximum(_dot(hn_ref[...], wup_ref[...]), 0.0)
    out_ref[...] += _dot((a * a).astype(_BF16), wdn_ref[...])

    if n_convert:
        @pl.when(i * pl.num_programs(1) + f < CONVERT_STEPS)
        def _convert():
            for src, dst in zip(convert_src, convert_dst):
                dst[...] = src[...].astype(_BF16)

    if final:
        last_f = f == pl.num_programs(1) - 1

        @pl.when(last_f)
        def _finish():
            out_ref[...] = _rms(out_ref[...], gf_ref[...])

        @pl.when(last_f & (i == N_MLP_TILES - 1))
        def _split():
            outs_ref[...] = out_ref[MLP_SAMPLE_ROW0:, :]


def _mlp_layer(layer, h, norm_mlp, w_up, w_down, norm_f=None, convert=()):
    final = norm_f is not None
    n_ff = D_FF // MLP_FF_TILE
    assert N_MLP_TILES * n_ff >= CONVERT_STEPS
    in_specs = [
        pl.BlockSpec((MLP_ROW_TILE, D_MODEL), lambda i, f: (i, 0)),
        pl.BlockSpec((None, 1, D_MODEL), lambda i, f: (layer, 0, 0)),
        pl.BlockSpec((D_MODEL, MLP_FF_TILE), lambda i, f: (0, f)),
        pl.BlockSpec((MLP_FF_TILE, D_MODEL), lambda i, f: (f, 0)),
    ]
    args = [h, norm_mlp, w_up, w_down]
    row_spec = pl.BlockSpec((MLP_ROW_TILE, D_MODEL), lambda i, f: (i, 0))
    if final:
        in_specs.append(pl.BlockSpec((1, D_MODEL), lambda i, f: (0, 0)))
        args.append(norm_f)
        out_specs = [row_spec, pl.BlockSpec((N_SAMPLE, D_MODEL), lambda i, f: (0, 0))]
        out_shape = [jax.ShapeDtypeStruct((N_PROMPT, D_MODEL), _F32),
                     jax.ShapeDtypeStruct((N_SAMPLE, D_MODEL), _F32)]
    else:
        out_specs = [row_spec]
        out_shape = [jax.ShapeDtypeStruct((N_ROWS, D_MODEL), _F32)]
    band = lambda i, f: jnp.minimum(i * n_ff + f, CONVERT_STEPS - 1)
    for stacked, idx in convert:
        _, rows, cols = stacked.shape
        rb = rows // CONVERT_STEPS
        assert rb * CONVERT_STEPS == rows and rb % 16 == 0
        in_specs.append(pl.BlockSpec((None, rb, cols), lambda i, f, idx=idx: (idx, band(i, f), 0)))
        out_specs.append(pl.BlockSpec((rb, cols), lambda i, f: (band(i, f), 0)))
        out_shape.append(jax.ShapeDtypeStruct((rows, cols), _BF16))
        args.append(stacked)
    return pl.pallas_call(
        functools.partial(_mlp_kernel, final=final, n_convert=len(convert)),
        grid=(N_MLP_TILES, n_ff),
        in_specs=in_specs,
        out_specs=out_specs,
        out_shape=out_shape,
        scratch_shapes=[pltpu.VMEM((MLP_ROW_TILE, D_MODEL), _BF16)],
        compiler_params=pltpu.CompilerParams(
            dimension_semantics=("arbitrary", "arbitrary"), vmem_limit_bytes=VMEM_LIMIT),
        name="mlp_layer",
    )(*args)


def _kv_kernel(h_ref, g_ref, w_ref, out_ref):
    hk = _rms(h_ref[...], g_ref[...]).astype(_BF16)
    out_ref[...] = _dot(hk, w_ref[...])


def _kv_proj(h, gain, w_kv):
    return pl.pallas_call(
        _kv_kernel,
        grid=(N_ROW_TILES,),
        in_specs=[
            pl.BlockSpec((ROW_TILE, D_MODEL), lambda i: (i, 0)),
            pl.BlockSpec((1, D_MODEL), lambda i: (0, 0)),
            pl.BlockSpec((D_MODEL, 2 * KV_DIM), lambda i: (0, 0)),
        ],
        out_specs=pl.BlockSpec((ROW_TILE, 2 * KV_DIM), lambda i: (i, 0)),
        out_shape=jax.ShapeDtypeStruct((N_ROWS, 2 * KV_DIM), _F32),
        compiler_params=pltpu.CompilerParams(
            dimension_semantics=("arbitrary",), vmem_limit_bytes=VMEM_LIMIT),
        name="kv_proj",
    )(h, gain, w_kv)


def _attn_core(q_groups, k_full, v_full, allowed, tq, sink_ref):
    kt = k_full.T.astype(_BF16)
    lane = lax.broadcasted_iota(jnp.int32, (KEY_SLOTS, 2 * HEAD_DIM), 1)
    low_half = lane < HEAD_DIM
    zk = jnp.zeros((HEAD_DIM, KEY_SLOTS), _BF16)
    outs = []
    for g in range(N_KV_HEADS):
        kgt = kt[g * HEAD_DIM:(g + 1) * HEAD_DIM, :]
        kk = jnp.concatenate([jnp.concatenate([kgt, zk], axis=1),
                              jnp.concatenate([zk, kgt], axis=1)], axis=0)
        c, e0 = divmod(g, 2)
        va = v_full[:, c * 2 * HEAD_DIM:(c + 1) * 2 * HEAD_DIM]
        vr = pltpu.roll(va, HEAD_DIM, 1)
        top = jnp.where(low_half, va if e0 == 0 else vr, 0.0)
        bot = jnp.where(low_half, 0.0, vr if e0 == 0 else va)
        vv = jnp.concatenate([top, bot], axis=0).astype(_BF16)
        qg = q_groups[g]
        qs = jnp.concatenate([qg[:, p * 128:(p + 1) * 128] for p in range(4)], axis=0)
        s = _dot(qs.astype(_BF16), kk)
        p_rows = []
        for p in range(4):
            p_cols = []
            for e in range(2):
                sink = sink_ref[g * 8 + p * 2 + e]
                spe = s[p * tq:(p + 1) * tq, e * KEY_SLOTS:(e + 1) * KEY_SLOTS]
                spe = jnp.where(allowed, spe, -jnp.inf)
                m = jnp.maximum(jnp.max(spe, axis=-1, keepdims=True), sink)
                pe = jnp.exp(spe - m)
                l = jnp.sum(pe, axis=-1, keepdims=True) + jnp.exp(sink - m)
                p_cols.append(pe * (1.0 / l))
            p_rows.append(jnp.concatenate(p_cols, axis=1))
        pm = jnp.concatenate(p_rows, axis=0).astype(_BF16)
        o = _dot(pm, vv)
        outs.append(jnp.concatenate([o[p * tq:(p + 1) * tq, :] for p in range(4)], axis=1))
    return outs


def _attn_kernel(sink_ref, h_ref, kv_ref, kvp_ref, ck_ref, cv_ref, g_ref, wq_ref, wo_ref,
                 out_ref, q_ref, o_ref):
    i = pl.program_id(0)
    x = h_ref[...]
    hn = _rms(x, g_ref[...]).astype(_BF16)
    q = _dot(hn, wq_ref[...]) * (HEAD_DIM ** -0.5)
    for g in range(N_KV_HEADS):
        q_ref[g] = q[:, g * GROUP_LANES:(g + 1) * GROUP_LANES]

    @pl.when(i < N_PROMPT_TILES)
    def _prompt():
        first = (i % TILES_PER_SEQ) == 0
        qi = lax.broadcasted_iota(jnp.int32, (WINDOW, KEY_SLOTS), 0)
        kj = lax.broadcasted_iota(jnp.int32, (WINDOW, KEY_SLOTS), 1)
        band = (kj > qi) & (kj <= qi + WINDOW)
        kv_cat = jnp.concatenate([kvp_ref[...], kv_ref[...]], axis=0)
        for n in range(ROW_TILE // WINDOW):
            allowed = band
            if n == 0:
                allowed = band & ((kj >= WINDOW) | jnp.logical_not(first))
            rows = slice(n * WINDOW, (n + 1) * WINDOW)
            kb = kv_cat[n * WINDOW:n * WINDOW + KEY_SLOTS, 0:KV_DIM]
            vb = kv_cat[n * WINDOW:n * WINDOW + KEY_SLOTS, KV_DIM:2 * KV_DIM]
            outs = _attn_core([q_ref[g, rows, :] for g in range(N_KV_HEADS)],
                              kb, vb, allowed, WINDOW, sink_ref)
            for g in range(N_KV_HEADS):
                o_ref[g, rows, :] = outs[g]

    @pl.when(i == N_PROMPT_TILES)
    def _sample():
        qi = lax.broadcasted_iota(jnp.int32, (DEC_SEQ, KEY_SLOTS), 0)
        kj = lax.broadcasted_iota(jnp.int32, (DEC_SEQ, KEY_SLOTS), 1)
        allowed = ((kj < WINDOW) & (kj > qi)) | ((kj >= WINDOW) & (kj - WINDOW <= qi))
        pad = jnp.zeros((KEY_SLOTS - WINDOW - DEC_SEQ, KV_DIM), _F32)

        def body(b, carry):
            rows = pl.ds(pl.multiple_of(b * DEC_SEQ, DEC_SEQ), DEC_SEQ)
            kb = jnp.concatenate([ck_ref[b], kv_ref[rows, 0:KV_DIM], pad], axis=0)
            vb = jnp.concatenate([cv_ref[b], kv_ref[rows, KV_DIM:2 * KV_DIM], pad], axis=0)
            outs = _attn_core([q_ref[g, rows, :] for g in range(N_KV_HEADS)],
                              kb, vb, allowed, DEC_SEQ, sink_ref)
            for g in range(N_KV_HEADS):
                o_ref[g, rows, :] = outs[g]
            return carry

        lax.fori_loop(0, DEC_BATCH, body, 0)

    acc = x
    for g in range(N_KV_HEADS):
        acc = acc + _dot(o_ref[g].astype(_BF16), wo_ref[g * GROUP_LANES:(g + 1) * GROUP_LANES, :])
    out_ref[...] = acc


def _attn_layer(layer, h, kv, cache_k, cache_v, norm_b, w_q, w_o, sinks):
    kv_blocks_per_tile = ROW_TILE // WINDOW
    const1 = pl.Buffered(1)
    return pl.pallas_call(
        _attn_kernel,
        grid_spec=pltpu.PrefetchScalarGridSpec(
            num_scalar_prefetch=0,
            grid=(N_ROW_TILES,),
            in_specs=[
                pl.BlockSpec(memory_space=pltpu.SMEM),
                pl.BlockSpec((ROW_TILE, D_MODEL), lambda i: (i, 0)),
                pl.BlockSpec((ROW_TILE, 2 * KV_DIM), lambda i: (i, 0)),
                pl.BlockSpec((WINDOW, 2 * KV_DIM),
                             lambda i: (jnp.maximum(i * kv_blocks_per_tile - 1, 0), 0)),
                pl.BlockSpec((DEC_BATCH, WINDOW, KV_DIM), lambda i: (0, 0, 0), pipeline_mode=const1),
                pl.BlockSpec((DEC_BATCH, WINDOW, KV_DIM), lambda i: (0, 0, 0), pipeline_mode=const1),
                pl.BlockSpec((None, 1, D_MODEL), lambda i: (layer, 0, 0)),
                pl.BlockSpec((D_MODEL, D_MODEL), lambda i: (0, 0), pipeline_mode=const1),
                pl.BlockSpec((D_MODEL, D_MODEL), lambda i: (0, 0), pipeline_mode=const1),
            ],
            out_specs=pl.BlockSpec((ROW_TILE, D_MODEL), lambda i: (i, 0)),
            scratch_shapes=[
                pltpu.VMEM((N_KV_HEADS, ROW_TILE, GROUP_LANES), _F32),
                pltpu.VMEM((N_KV_HEADS, ROW_TILE, GROUP_LANES), _F32),
            ],
        ),
        out_shape=jax.ShapeDtypeStruct((N_ROWS, D_MODEL), _F32),
        compiler_params=pltpu.CompilerParams(
            dimension_semantics=("arbitrary",), vmem_limit_bytes=VMEM_LIMIT),
        name="attn_layer",
    )(sinks, h, kv, kv, cache_k, cache_v, norm_b, w_q, w_o)


def kernel(x_prompt, x_sample, state_pool, cache_k_win, cache_v_win, norm_a, w_pool, pool_scale,
           norm_kv, w_k, w_v, norm_b, w_q, w_o, sinks, norm_mlp, w_up, w_down, norm_f):
    cache_k = cache_k_win.reshape(DEC_BATCH, WINDOW, KV_DIM)
    cache_v = cache_v_win.reshape(DEC_BATCH, WINDOW, KV_DIM)
    rows3 = lambda v: v.reshape(v.shape[0], 1, D_MODEL)
    norm_a3, scale3, norm_b3, norm_mlp3 = rows3(norm_a), rows3(pool_scale), rows3(norm_b), rows3(norm_mlp)
    w_pool_b = w_pool.astype(_BF16)
    w_kv = jnp.concatenate([w_k, w_v], axis=1).astype(_BF16)
    w_up_b, w_down_b = w_up[0].astype(_BF16), w_down[0].astype(_BF16)
    w_q_b = w_o_b = kv = None

    pool_p, pool_s = [], []
    h_prompt = x_prompt.reshape(N_PROMPT, D_MODEL)
    h_sample, sample_tile = x_sample.reshape(N_SAMPLE, D_MODEL), 0
    for l in range(DEPTH):
        if l < N_A_LAYERS:
            h, pp, ps = _pool_layer(l, h_prompt, h_sample, sample_tile, state_pool, norm_a3,
                                    w_pool_b, scale3)
            pool_p.append(pp)
            pool_s.append(ps)
        else:
            h = _attn_layer(l - N_A_LAYERS, h, kv, cache_k, cache_v, norm_b3, w_q_b, w_o_b,
                            sinks[l - N_A_LAYERS])
        if l == DEPTH - 1:
            y_prompt, y_sample = _mlp_layer(l, h, norm_mlp3, w_up_b, w_down_b,
                                            norm_f=norm_f.reshape(1, D_MODEL))
            break
        convert = [(w_up, l + 1), (w_down, l + 1)]
        if l + 1 >= N_A_LAYERS:
            convert += [(w_q, l + 1 - N_A_LAYERS), (w_o, l + 1 - N_A_LAYERS)]
        h, w_up_b, w_down_b, *w_attn = _mlp_layer(l, h, norm_mlp3, w_up_b, w_down_b, convert=convert)
        if w_attn:
            w_q_b, w_o_b = w_attn
        h_prompt, h_sample, sample_tile = h, h, N_PROMPT_TILES
        if l == N_A_LAYERS - 1:
            kv = _kv_proj(h, norm_kv.reshape(1, D_MODEL), w_kv)

    k_all, v_all = kv[:, :KV_DIM], kv[:, KV_DIM:]

    def prompt_window(a):
        return a[:N_PROMPT].reshape(BATCH, SEQ, N_KV_HEADS, HEAD_DIM)[:, SEQ - WINDOW:]

    def sample_window(buf, a):
        new = a[N_PROMPT:].reshape(DEC_BATCH, DEC_SEQ, N_KV_HEADS, HEAD_DIM)
        return jnp.concatenate([buf, new], axis=1)[:, DEC_SEQ:]

    return (y_prompt.reshape(BATCH, SEQ, D_MODEL), y_sample.reshape(DEC_BATCH, DEC_SEQ, D_MODEL),
            jnp.stack(pool_p), jnp.stack(pool_s),
            prompt_window(k_all), prompt_window(v_all),
            sample_window(cache_k_win, k_all), sample_window(cache_v_win, v_all))
```

```python
import functools

import jax
import jax.numpy as jnp
from jax import lax
from jax.experimental import pallas as pl
from jax.experimental.pallas import tpu as pltpu

D_MODEL = 2048
BATCH = 4
SEQ = 2048
DEC_BATCH = 32
DEC_SEQ = 8
N_A_LAYERS = 2
N_B_LAYERS = 2
DEPTH = 4
POOL_WINDOWS = (2, 4, 8, 16)
POOL_GROUP = D_MODEL // len(POOL_WINDOWS)
POOL_STATE = 15
HEAD_DIM = 64
N_HEADS = 32
N_KV_HEADS = 4
KV_DIM = N_KV_HEADS * HEAD_DIM
GROUP_LANES = D_MODEL // N_KV_HEADS
WINDOW = 128
D_FF = 4 * D_MODEL
RMS_EPS = 1e-5

N_PROMPT = BATCH * SEQ
N_SAMPLE = DEC_BATCH * DEC_SEQ
N_ROWS = N_PROMPT + N_SAMPLE

ROW_TILE = 256
N_ROW_TILES = N_ROWS // ROW_TILE
TILES_PER_SEQ = SEQ // ROW_TILE
N_PROMPT_TILES = N_PROMPT // ROW_TILE
HALO = 16
MLP_ROW_TILE = 768
MLP_FF_TILE = 1024
N_MLP_TILES = N_ROWS // MLP_ROW_TILE
MLP_SAMPLE_ROW0 = N_PROMPT - (N_MLP_TILES - 1) * MLP_ROW_TILE
assert N_MLP_TILES * MLP_ROW_TILE == N_ROWS and MLP_SAMPLE_ROW0 + N_SAMPLE == MLP_ROW_TILE
KEY_SLOTS = 2 * WINDOW
SAMPLE_UNROLL = 4
CONVERT_STEPS = 64

VMEM_LIMIT = 56 * 1024 * 1024

_BF16 = jnp.bfloat16
_F32 = jnp.float32


def _rms(x, g):
    ms = jnp.mean(x * x, axis=-1, keepdims=True)
    return (x * lax.rsqrt(ms + RMS_EPS)) * g


def _dot(a, b):
    return jnp.dot(a, b, preferred_element_type=_F32)


def _pool_kernel(hp_ref, halo_ref, hs_ref, past_ref, g_ref, w_ref, sc_ref,
                 out_ref, poolp_ref, pools_ref, ext_ref, ext3_ref):
    i = pl.program_id(0)
    g = g_ref[...]

    @pl.when(i < N_PROMPT_TILES)
    def _prompt():
        x = hp_ref[...]
        u = _rms(x, g)
        first = (i % TILES_PER_SEQ) == 0
        halo = jnp.where(first, 0.0, _rms(halo_ref[...], g))
        ext_ref[0:HALO, :] = halo
        ext_ref[HALO:HALO + ROW_TILE, :] = u
        poolp_ref[0] = ext_ref[HALO + ROW_TILE - POOL_STATE:HALO + ROW_TILE, :]
        pos = (i % TILES_PER_SEQ) * ROW_TILE + lax.broadcasted_iota(jnp.int32, (ROW_TILE, 1), 0)
        for gi, w in enumerate(POOL_WINDOWS):
            sl = slice(gi * POOL_GROUP, (gi + 1) * POOL_GROUP)
            ug = u[:, sl]
            s = ug
            for j in range(1, w):
                s = s + ext_ref[HALO - j:HALO - j + ROW_TILE, sl]
            inv = 1.0 / jnp.minimum(pos + 1, w).astype(_F32)
            d = s * inv - ug
            o = _dot(d.astype(_BF16), w_ref[gi])
            out_ref[:, sl] = x[:, sl] + o * sc_ref[:, sl]

    @pl.when(i == N_PROMPT_TILES)
    def _sample():
        x = hs_ref[...]
        u = _rms(x, g)
        u3 = u.reshape(DEC_BATCH, DEC_SEQ, D_MODEL)
        ext3_ref[:, HALO - POOL_STATE:HALO, :] = past_ref[...]
        ext3_ref[:, HALO:HALO + DEC_SEQ, :] = u3
        pools_ref[:, 0:POOL_STATE - DEC_SEQ, :] = past_ref[:, DEC_SEQ:POOL_STATE, :]
        pools_ref[:, POOL_STATE - DEC_SEQ:POOL_STATE, :] = u3
        for gi, w in enumerate(POOL_WINDOWS):
            sl = slice(gi * POOL_GROUP, (gi + 1) * POOL_GROUP)
            ug = u3[:, :, sl]
            s = ug
            for j in range(1, w):
                s = s + ext3_ref[:, HALO - j:HALO - j + DEC_SEQ, sl]
            d = s * (1.0 / w) - ug
            o = _dot(d.reshape(N_SAMPLE, POOL_GROUP).astype(_BF16), w_ref[gi])
            out_ref[:, sl] = x[:, sl] + o * sc_ref[:, sl]


def _pool_layer(layer, h_prompt, h_sample, sample_tile, state_pool, norm_a, w_pool, pool_scale):
    halo_blocks_per_tile = ROW_TILE // HALO
    return pl.pallas_call(
        _pool_kernel,
        grid=(N_ROW_TILES,),
        in_specs=[
            pl.BlockSpec((ROW_TILE, D_MODEL), lambda i: (jnp.minimum(i, N_PROMPT_TILES - 1), 0)),
            pl.BlockSpec((HALO, D_MODEL),
                         lambda i: (jnp.clip(i * halo_blocks_per_tile - 1, 0,
                                             N_PROMPT // HALO - 1), 0)),
            pl.BlockSpec((ROW_TILE, D_MODEL), lambda i: (sample_tile, 0)),
            pl.BlockSpec((None, DEC_BATCH, POOL_STATE, D_MODEL), lambda i: (layer, 0, 0, 0),
                         pipeline_mode=pl.Buffered(1)),
            pl.BlockSpec((None, 1, D_MODEL), lambda i: (layer, 0, 0)),
            pl.BlockSpec((None, len(POOL_WINDOWS), POOL_GROUP, POOL_GROUP),
                         lambda i: (layer, 0, 0, 0)),
            pl.BlockSpec((None, 1, D_MODEL), lambda i: (layer, 0, 0)),
        ],
        out_specs=[
            pl.BlockSpec((ROW_TILE, D_MODEL), lambda i: (i, 0)),
            pl.BlockSpec((1, POOL_STATE, D_MODEL),
                         lambda i: (jnp.minimum(i // TILES_PER_SEQ, BATCH - 1), 0, 0)),
            pl.BlockSpec((DEC_BATCH, POOL_STATE, D_MODEL), lambda i: (0, 0, 0)),
        ],
        out_shape=[
            jax.ShapeDtypeStruct((N_ROWS, D_MODEL), _F32),
            jax.ShapeDtypeStruct((BATCH, POOL_STATE, D_MODEL), _F32),
            jax.ShapeDtypeStruct((DEC_BATCH, POOL_STATE, D_MODEL), _F32),
        ],
        scratch_shapes=[
            pltpu.VMEM((HALO + ROW_TILE, D_MODEL), _F32),
            pltpu.VMEM((DEC_BATCH, HALO + DEC_SEQ, D_MODEL), _F32),
        ],
        compiler_params=pltpu.CompilerParams(
            dimension_semantics=("arbitrary",), vmem_limit_bytes=VMEM_LIMIT),
        name="pool_layer",
    )(h_prompt, h_prompt, h_sample, state_pool, norm_a, w_pool, pool_scale)


def _mlp_kernel(*refs, final, n_convert):
    h_ref, g_ref, wup_ref, wdn_ref = refs[:4]
    n_in = 4 + int(final) + n_convert
    gf_ref = refs[4] if final else None
    convert_src = refs[n_in - n_convert:n_in]
    out_ref = refs[n_in]
    outs_ref = refs[n_in + 1] if final else None
    n_out = 1 + int(final)
    convert_dst = refs[n_in + n_out:n_in + n_out + n_convert]
    hn_ref = refs[n_in + n_out + n_convert]
    i = pl.program_id(0)
    f = pl.program_id(1)

    @pl.when(f == 0)
    def _init():
        x = h_ref[...]
        hn_ref[...] = _rms(x, g_ref[...]).astype(_BF16)
        out_ref[...] = x

    a = jnp.maximum(_dot(hn_ref[...], wup_ref[...]), 0.0)
    out_ref[...] += _dot((a * a).astype(_BF16), wdn_ref[...])

    if n_convert:
        @pl.when(i * pl.num_programs(1) + f < CONVERT_STEPS)
        def _convert():
            for src, dst in zip(convert_src, convert_dst):
                dst[...] = src[...].astype(_BF16)

    if final:
        last_f = f == pl.num_programs(1) - 1

        @pl.when(last_f)
        def _finish():
            out_ref[...] = _rms(out_ref[...], gf_ref[...])

        @pl.when(last_f & (i == N_MLP_TILES - 1))
        def _split():
            outs_ref[...] = out_ref[MLP_SAMPLE_ROW0:, :]


def _mlp_layer(layer, h, norm_mlp, w_up, w_down, norm_f=None, convert=()):
    final = norm_f is not None
    n_ff = D_FF // MLP_FF_TILE
    assert N_MLP_TILES * n_ff >= CONVERT_STEPS
    in_specs = [
        pl.BlockSpec((MLP_ROW_TILE, D_MODEL), lambda i, f: (i, 0)),
        pl.BlockSpec((None, 1, D_MODEL), lambda i, f: (layer, 0, 0)),
        pl.BlockSpec((D_MODEL, MLP_FF_TILE), lambda i, f: (0, f)),
        pl.BlockSpec((MLP_FF_TILE, D_MODEL), lambda i, f: (f, 0)),
    ]
    args = [h, norm_mlp, w_up, w_down]
    row_spec = pl.BlockSpec((MLP_ROW_TILE, D_MODEL), lambda i, f: (i, 0))
    if final:
        in_specs.append(pl.BlockSpec((1, D_MODEL), lambda i, f: (0, 0)))
        args.append(norm_f)
        out_specs = [row_spec, pl.BlockSpec((N_SAMPLE, D_MODEL), lambda i, f: (0, 0))]
        out_shape = [jax.ShapeDtypeStruct((N_PROMPT, D_MODEL), _F32),
                     jax.ShapeDtypeStruct((N_SAMPLE, D_MODEL), _F32)]
    else:
        out_specs = [row_spec]
        out_shape = [jax.ShapeDtypeStruct((N_ROWS, D_MODEL), _F32)]
    band = lambda i, f: jnp.minimum(i * n_ff + f, CONVERT_STEPS - 1)
    for stacked, idx in convert:
        _, rows, cols = stacked.shape
        rb = rows // CONVERT_STEPS
        assert rb * CONVERT_STEPS == rows and rb % 16 == 0
        in_specs.append(pl.BlockSpec((None, rb, cols), lambda i, f, idx=idx: (idx, band(i, f), 0)))
        out_specs.append(pl.BlockSpec((rb, cols), lambda i, f: (band(i, f), 0)))
        out_shape.append(jax.ShapeDtypeStruct((rows, cols), _BF16))
        args.append(stacked)
    return pl.pallas_call(
        functools.partial(_mlp_kernel, final=final, n_convert=len(convert)),
        grid=(N_MLP_TILES, n_ff),
        in_specs=in_specs,
        out_specs=out_specs,
        out_shape=out_shape,
        scratch_shapes=[pltpu.VMEM((MLP_ROW_TILE, D_MODEL), _BF16)],
        compiler_params=pltpu.CompilerParams(
            dimension_semantics=("arbitrary", "arbitrary"), vmem_limit_bytes=VMEM_LIMIT),
        name="mlp_layer",
    )(*args)


def _kv_kernel(h_ref, g_ref, w_ref, out_ref):
    hk = _rms(h_ref[...], g_ref[...]).astype(_BF16)
    out_ref[...] = _dot(hk, w_ref[...])


def _kv_proj(h, gain, w_kv):
    return pl.pallas_call(
        _kv_kernel,
        grid=(N_ROW_TILES,),
        in_specs=[
            pl.BlockSpec((ROW_TILE, D_MODEL), lambda i: (i, 0)),
            pl.BlockSpec((1, D_MODEL), lambda i: (0, 0)),
            pl.BlockSpec((D_MODEL, 2 * KV_DIM), lambda i: (0, 0)),
        ],
        out_specs=pl.BlockSpec((ROW_TILE, 2 * KV_DIM), lambda i: (i, 0)),
        out_shape=jax.ShapeDtypeStruct((N_ROWS, 2 * KV_DIM), _F32),
        compiler_params=pltpu.CompilerParams(
            dimension_semantics=("arbitrary",), vmem_limit_bytes=VMEM_LIMIT),
        name="kv_proj",
    )(h, gain, w_kv)


def _pair_diag_tall(x_full, g):
    keys = x_full.shape[0]
    c, e0 = divmod(g, 2)
    xa = x_full[:, c * 2 * HEAD_DIM:(c + 1) * 2 * HEAD_DIM]
    xr = pltpu.roll(xa, HEAD_DIM, 1)
    low_half = lax.broadcasted_iota(jnp.int32, (keys, 2 * HEAD_DIM), 1) < HEAD_DIM
    top = jnp.where(low_half, xa if e0 == 0 else xr, 0.0)
    bot = jnp.where(low_half, 0.0, xr if e0 == 0 else xa)
    return jnp.concatenate([top, bot], axis=0).astype(_BF16)


def _pair_diag_wide(xt, g):
    xg = xt[g * HEAD_DIM:(g + 1) * HEAD_DIM, :]
    z = jnp.zeros_like(xg)
    return jnp.concatenate([jnp.concatenate([xg, z], axis=1),
                            jnp.concatenate([z, xg], axis=1)], axis=0)


def _rows_scores(q_groups, k_full):
    kt = k_full.T.astype(_BF16)
    scores = []
    for g in range(N_KV_HEADS):
        qg = q_groups[g]
        qs = jnp.concatenate([qg[:, p * 128:(p + 1) * 128] for p in range(4)], axis=0)
        scores.append(_dot(qs.astype(_BF16), _pair_diag_wide(kt, g)))
    return scores


def _rows_softmax(scores, allowed, tq, sink_ref):
    probs = []
    for g in range(N_KV_HEADS):
        s = scores[g]
        p_rows = []
        for p in range(4):
            p_cols = []
            for e in range(2):
                sink = sink_ref[g * 8 + p * 2 + e]
                spe = s[p * tq:(p + 1) * tq, e * KEY_SLOTS:(e + 1) * KEY_SLOTS]
                spe = jnp.where(allowed, spe, -jnp.inf)
                m = jnp.maximum(jnp.max(spe, axis=-1, keepdims=True), sink)
                pe = jnp.exp(spe - m)
                l = jnp.sum(pe, axis=-1, keepdims=True) + jnp.exp(sink - m)
                p_cols.append(pe * (1.0 / l))
            p_rows.append(jnp.concatenate(p_cols, axis=1))
        probs.append(jnp.concatenate(p_rows, axis=0).astype(_BF16))
    return probs


def _rows_values(probs, v_full, tq):
    outs = []
    for g in range(N_KV_HEADS):
        o = _dot(probs[g], _pair_diag_tall(v_full, g))
        outs.append(jnp.concatenate([o[p * tq:(p + 1) * tq, :] for p in range(4)], axis=1))
    return outs


def _attn_sample_kernel(sink_ref, h_ref, kv_ref, ck_ref, cv_ref, g_ref, wq_ref, wo_ref,
                        out_ref, q_ref, o_ref):
    x = h_ref[...]
    hn = _rms(x, g_ref[...]).astype(_BF16)
    q = _dot(hn, wq_ref[...]) * (HEAD_DIM ** -0.5)
    for g in range(N_KV_HEADS):
        q_ref[g] = q[:, g * GROUP_LANES:(g + 1) * GROUP_LANES]

    qi = lax.broadcasted_iota(jnp.int32, (DEC_SEQ, KEY_SLOTS), 0)
    kj = lax.broadcasted_iota(jnp.int32, (DEC_SEQ, KEY_SLOTS), 1)
    allowed = ((kj < WINDOW) & (kj > qi)) | ((kj >= WINDOW) & (kj - WINDOW <= qi))
    pad = jnp.zeros((KEY_SLOTS - WINDOW - DEC_SEQ, KV_DIM), _F32)

    def body(j, carry):
        seqs = []
        for u in range(SAMPLE_UNROLL):
            b = j * SAMPLE_UNROLL + u
            rows = pl.ds(pl.multiple_of(b * DEC_SEQ, DEC_SEQ), DEC_SEQ)
            kb = jnp.concatenate([ck_ref[b], kv_ref[rows, 0:KV_DIM], pad], axis=0)
            scores = _rows_scores([q_ref[g, rows, :] for g in range(N_KV_HEADS)], kb)
            seqs.append((b, rows, scores))
        probs = [_rows_softmax(scores, allowed, DEC_SEQ, sink_ref) for _, _, scores in seqs]
        for (b, rows, _), pm in zip(seqs, probs):
            vb = jnp.concatenate([cv_ref[b], kv_ref[rows, KV_DIM:2 * KV_DIM], pad], axis=0)
            outs = _rows_values(pm, vb, DEC_SEQ)
            for g in range(N_KV_HEADS):
                o_ref[g, rows, :] = outs[g]
        return carry

    lax.fori_loop(0, DEC_BATCH // SAMPLE_UNROLL, body, 0)

    acc = x
    for g in range(N_KV_HEADS):
        acc = acc + _dot(o_ref[g].astype(_BF16), wo_ref[g * GROUP_LANES:(g + 1) * GROUP_LANES, :])
    out_ref[...] = acc


def _attn_sample(layer, h, kv, cache_k, cache_v, norm_b, w_q, w_o, sinks):
    const1 = pl.Buffered(1)
    return pl.pallas_call(
        _attn_sample_kernel,
        grid=(1,),
        in_specs=[
            pl.BlockSpec(memory_space=pltpu.SMEM),
            pl.BlockSpec((ROW_TILE, D_MODEL), lambda i: (N_PROMPT_TILES, 0), pipeline_mode=const1),
            pl.BlockSpec((ROW_TILE, 2 * KV_DIM), lambda i: (N_PROMPT_TILES, 0), pipeline_mode=const1),
            pl.BlockSpec((DEC_BATCH, WINDOW, KV_DIM), lambda i: (0, 0, 0), pipeline_mode=const1),
            pl.BlockSpec((DEC_BATCH, WINDOW, KV_DIM), lambda i: (0, 0, 0), pipeline_mode=const1),
            pl.BlockSpec((None, 1, D_MODEL), lambda i: (layer, 0, 0)),
            pl.BlockSpec((D_MODEL, D_MODEL), lambda i: (0, 0), pipeline_mode=const1),
            pl.BlockSpec((D_MODEL, D_MODEL), lambda i: (0, 0), pipeline_mode=const1),
        ],
        out_specs=pl.BlockSpec((N_SAMPLE, D_MODEL), lambda i: (0, 0)),
        out_shape=jax.ShapeDtypeStruct((N_SAMPLE, D_MODEL), _F32),
        scratch_shapes=[
            pltpu.VMEM((N_KV_HEADS, N_SAMPLE, GROUP_LANES), _F32),
            pltpu.VMEM((N_KV_HEADS, N_SAMPLE, GROUP_LANES), _F32),
        ],
        compiler_params=pltpu.CompilerParams(
            dimension_semantics=("arbitrary",), vmem_limit_bytes=VMEM_LIMIT),
        name="attn_sample",
    )(sinks, h, kv, cache_k, cache_v, norm_b, w_q, w_o)


def _attn_core_lanes(qt_ref, ot_ref, cols, k_full, v_full, allowed_t, sink_ref, between):
    vt = v_full.T.astype(_BF16)

    def scores(g):
        base = g * GROUP_LANES
        qst = jnp.concatenate([qt_ref[base + p * 128:base + (p + 1) * 128, cols]
                               for p in range(4)], axis=1)
        return _dot(_pair_diag_tall(k_full, g), qst)

    def softmax(g, st):
        inv = {}
        pt_rows = []
        for e in range(2):
            pt_cols = []
            for p in range(4):
                sink = sink_ref[g * 8 + p * 2 + e]
                sub = st[e * KEY_SLOTS:(e + 1) * KEY_SLOTS, p * WINDOW:(p + 1) * WINDOW]
                sub = jnp.where(allowed_t, sub, -jnp.inf)
                m = jnp.maximum(jnp.max(sub, axis=0, keepdims=True), sink)
                pe = jnp.exp(sub - m)
                l = jnp.sum(pe, axis=0, keepdims=True) + jnp.exp(sink - m)
                inv[e, p] = 1.0 / l
                pt_cols.append(pe.astype(_BF16))
            pt_rows.append(jnp.concatenate(pt_cols, axis=1))
        return jnp.concatenate(pt_rows, axis=0), inv

    def values(g, pt, inv):
        base = g * GROUP_LANES
        ot = _dot(_pair_diag_wide(vt, g), pt)
        for e in range(2):
            for p in range(4):
                r0 = base + p * 128 + e * HEAD_DIM
                ot_ref[r0:r0 + HEAD_DIM, cols] = (
                    ot[e * HEAD_DIM:(e + 1) * HEAD_DIM, p * WINDOW:(p + 1) * WINDOW] * inv[e, p])

    st = {0: scores(0), 1: scores(1)}
    between()
    for g in range(N_KV_HEADS):
        pt, inv = softmax(g, st.pop(g))
        if g + 2 < N_KV_HEADS:
            st[g + 2] = scores(g + 2)
        values(g, pt, inv)


def _attn_prompt_kernel(sink_ref, h_ref, kv_ref, kvp_ref, hs_ref, g_ref, wq_ref, wo_ref,
                        out_ref, qt_ref, ot_ref):
    i = pl.program_id(0)

    @pl.when(i < N_PROMPT_TILES)
    def _prompt():
        n_blocks = ROW_TILE // WINDOW
        cols = [slice(n * WINDOW, (n + 1) * WINDOW) for n in range(n_blocks)]
        hn = _rms(h_ref[...], g_ref[...]).astype(_BF16)

        def project_q(n):
            q = _dot(hn[cols[n], :], wq_ref[...]) * (HEAD_DIM ** -0.5)
            qt_ref[:, cols[n]] = q.T.astype(_BF16)

        def project_o(n):
            o = ot_ref[:, cols[n]].T.astype(_BF16)
            out_ref[cols[n], :] = h_ref[cols[n], :] + _dot(o, wo_ref[...])

        def between_blocks(n):
            if n > 0:
                project_o(n - 1)
            if n + 1 < n_blocks:
                project_q(n + 1)

        first = (i % TILES_PER_SEQ) == 0
        kj = lax.broadcasted_iota(jnp.int32, (KEY_SLOTS, WINDOW), 0)
        qi = lax.broadcasted_iota(jnp.int32, (KEY_SLOTS, WINDOW), 1)
        band = (kj > qi) & (kj <= qi + WINDOW)
        kv_cat = jnp.concatenate([kvp_ref[...], kv_ref[...]], axis=0)
        project_q(0)
        for n in range(n_blocks):
            allowed_t = band
            if n == 0:
                allowed_t = band & ((kj >= WINDOW) | jnp.logical_not(first))
            kb = kv_cat[n * WINDOW:n * WINDOW + KEY_SLOTS, 0:KV_DIM]
            vb = kv_cat[n * WINDOW:n * WINDOW + KEY_SLOTS, KV_DIM:2 * KV_DIM]
            _attn_core_lanes(qt_ref, ot_ref, cols[n], kb, vb, allowed_t, sink_ref,
                             functools.partial(between_blocks, n))
        project_o(n_blocks - 1)

    @pl.when(i == N_PROMPT_TILES)
    def _sample_rows():
        out_ref[...] = hs_ref[...]


def _attn_layer(layer, h, kv, cache_k, cache_v, norm_b, w_q, w_o, sinks):
    h_sample = _attn_sample(layer, h, kv, cache_k, cache_v, norm_b, w_q, w_o, sinks)
    kv_blocks_per_tile = ROW_TILE // WINDOW
    const1 = pl.Buffered(1)
    return pl.pallas_call(
        _attn_prompt_kernel,
        grid=(N_ROW_TILES,),
        in_specs=[
            pl.BlockSpec(memory_space=pltpu.SMEM),
            pl.BlockSpec((ROW_TILE, D_MODEL), lambda i: (i, 0)),
            pl.BlockSpec((ROW_TILE, 2 * KV_DIM), lambda i: (i, 0)),
            pl.BlockSpec((WINDOW, 2 * KV_DIM),
                         lambda i: (jnp.maximum(i * kv_blocks_per_tile - 1, 0), 0)),
            pl.BlockSpec((N_SAMPLE, D_MODEL), lambda i: (0, 0), pipeline_mode=const1),
            pl.BlockSpec((None, 1, D_MODEL), lambda i: (layer, 0, 0)),
            pl.BlockSpec((D_MODEL, D_MODEL), lambda i: (0, 0), pipeline_mode=const1),
            pl.BlockSpec((D_MODEL, D_MODEL), lambda i: (0, 0), pipeline_mode=const1),
        ],
        out_specs=pl.BlockSpec((ROW_TILE, D_MODEL), lambda i: (i, 0)),
        out_shape=jax.ShapeDtypeStruct((N_ROWS, D_MODEL), _F32),
        scratch_shapes=[
            pltpu.VMEM((D_MODEL, ROW_TILE), _BF16),
            pltpu.VMEM((D_MODEL, ROW_TILE), _F32),
        ],
        compiler_params=pltpu.CompilerParams(
            dimension_semantics=("arbitrary",), vmem_limit_bytes=VMEM_LIMIT),
        name="attn_prompt",
    )(sinks, h, kv, kv, h_sample, norm_b, w_q, w_o)


def kernel(x_prompt, x_sample, state_pool, cache_k_win, cache_v_win, norm_a, w_pool, pool_scale,
           norm_kv, w_k, w_v, norm_b, w_q, w_o, sinks, norm_mlp, w_up, w_down, norm_f):
    cache_k = cache_k_win.reshape(DEC_BATCH, WINDOW, KV_DIM)
    cache_v = cache_v_win.reshape(DEC_BATCH, WINDOW, KV_DIM)
    rows3 = lambda v: v.reshape(v.shape[0], 1, D_MODEL)
    norm_a3, scale3, norm_b3, norm_mlp3 = rows3(norm_a), rows3(pool_scale), rows3(norm_b), rows3(norm_mlp)
    w_pool_b = w_pool.astype(_BF16)
    w_kv = jnp.concatenate([w_k, w_v], axis=1).astype(_BF16)
    w_up_b, w_down_b = w_up[0].astype(_BF16), w_down[0].astype(_BF16)
    w_q_b = w_o_b = kv = None

    pool_p, pool_s = [], []
    h_prompt = x_prompt.reshape(N_PROMPT, D_MODEL)
    h_sample, sample_tile = x_sample.reshape(N_SAMPLE, D_MODEL), 0
    for l in range(DEPTH):
        if l < N_A_LAYERS:
            h, pp, ps = _pool_layer(l, h_prompt, h_sample, sample_tile, state_pool, norm_a3,
                                    w_pool_b, scale3)
            pool_p.append(pp)
            pool_s.append(ps)
        else:
            h = _attn_layer(l - N_A_LAYERS, h, kv, cache_k, cache_v, norm_b3, w_q_b, w_o_b,
                            sinks[l - N_A_LAYERS])
        if l == DEPTH - 1:
            y_prompt, y_sample = _mlp_layer(l, h, norm_mlp3, w_up_b, w_down_b,
                                            norm_f=norm_f.reshape(1, D_MODEL))
            break
        convert = [(w_up, l + 1), (w_down, l + 1)]
        if l + 1 >= N_A_LAYERS:
            convert += [(w_q, l + 1 - N_A_LAYERS), (w_o, l + 1 - N_A_LAYERS)]
        h, w_up_b, w_down_b, *w_attn = _mlp_layer(l, h, norm_mlp3, w_up_b, w_down_b, convert=convert)
        if w_attn:
            w_q_b, w_o_b = w_attn
        h_prompt, h_sample, sample_tile = h, h, N_PROMPT_TILES
        if l == N_A_LAYERS - 1:
            kv = _kv_proj(h, norm_kv.reshape(1, D_MODEL), w_kv)

    k_all, v_all = kv[:, :KV_DIM], kv[:, KV_DIM:]

    def prompt_window(a):
        return a[:N_PROMPT].reshape(BATCH, SEQ, N_KV_HEADS, HEAD_DIM)[:, SEQ - WINDOW:]

    def sample_window(buf, a):
        new = a[N_PROMPT:].reshape(DEC_BATCH, DEC_SEQ, N_KV_HEADS, HEAD_DIM)
        return jnp.concatenate([buf, new], axis=1)[:, DEC_SEQ:]

    return (y_prompt.reshape(BATCH, SEQ, D_MODEL), y_sample.reshape(DEC_BATCH, DEC_SEQ, D_MODEL),
            jnp.stack(pool_p), jnp.stack(pool_s),
            prompt_window(k_all), prompt_window(v_all),
            sample_window(cache_k_win, k_all), sample_window(cache_v_win, v_all))
```

```python
import functools

import jax
import jax.numpy as jnp
from jax import lax
from jax.experimental import pallas as pl
from jax.experimental.pallas import tpu as pltpu

D_MODEL = 2048
BATCH = 4
SEQ = 2048
DEC_BATCH = 32
DEC_SEQ = 8
N_A_LAYERS = 2
N_B_LAYERS = 2
DEPTH = 4
POOL_WINDOWS = (2, 4, 8, 16)
POOL_GROUP = D_MODEL // len(POOL_WINDOWS)
POOL_STATE = 15
HEAD_DIM = 64
N_HEADS = 32
N_KV_HEADS = 4
KV_DIM = N_KV_HEADS * HEAD_DIM
GROUP_LANES = D_MODEL // N_KV_HEADS
WINDOW = 128
D_FF = 4 * D_MODEL
RMS_EPS = 1e-5

N_PROMPT = BATCH * SEQ
N_SAMPLE = DEC_BATCH * DEC_SEQ
N_ROWS = N_PROMPT + N_SAMPLE

ROW_TILE = 256
N_ROW_TILES = N_ROWS // ROW_TILE
TILES_PER_SEQ = SEQ // ROW_TILE
N_PROMPT_TILES = N_PROMPT // ROW_TILE
HALO = 16
MLP_ROW_TILE = 768
MLP_FF_TILE = 1024
N_MLP_TILES = N_ROWS // MLP_ROW_TILE
N_FF_TILES = D_FF // MLP_FF_TILE
MLP_SAMPLE_ROW0 = N_PROMPT - (N_MLP_TILES - 1) * MLP_ROW_TILE
assert N_MLP_TILES * MLP_ROW_TILE == N_ROWS and MLP_SAMPLE_ROW0 + N_SAMPLE == MLP_ROW_TILE
assert N_FF_TILES % 2 == 0
KEY_SLOTS = 2 * WINDOW
SAMPLE_UNROLL = 4
CONVERT_STEPS = N_PROMPT_TILES

VMEM_LIMIT = 56 * 1024 * 1024

_BF16 = jnp.bfloat16
_F32 = jnp.float32


def _rms(x, g):
    ms = jnp.mean(x * x, axis=-1, keepdims=True)
    return (x * lax.rsqrt(ms + RMS_EPS)) * g


def _dot(a, b):
    return jnp.dot(a, b, preferred_element_type=_F32)


def _with_convert(body, n_in, n_out, n_convert):
    def kernel(*refs):
        a, b, c, d = n_in, n_in + n_convert, n_in + n_convert + n_out, n_in + 2 * n_convert + n_out
        if n_convert:
            @pl.when(pl.program_id(0) < CONVERT_STEPS)
            def _convert():
                for src, dst in zip(refs[a:b], refs[c:d]):
                    dst[...] = src[...].astype(_BF16)
        body(*refs[:a], *refs[b:c], *refs[d:])
    return kernel


def _convert_specs(convert):
    in_specs, args, out_specs, out_shape = [], [], [], []
    band = lambda i: jnp.minimum(i, CONVERT_STEPS - 1)
    for stacked, idx in convert:
        _, rows, cols = stacked.shape
        rb = rows // CONVERT_STEPS
        assert rb * CONVERT_STEPS == rows and rb % 16 == 0
        in_specs.append(pl.BlockSpec((None, rb, cols), lambda i, idx=idx: (idx, band(i), 0)))
        out_specs.append(pl.BlockSpec((rb, cols), lambda i: (band(i), 0)))
        out_shape.append(jax.ShapeDtypeStruct((rows, cols), _BF16))
        args.append(stacked)
    return in_specs, args, out_specs, out_shape


def _pool_kernel(hp_ref, halo_ref, hs_ref, past_ref, g_ref, w_ref, sc_ref,
                 out_ref, poolp_ref, pools_ref, ext_ref, ext3_ref):
    i = pl.program_id(0)
    g = g_ref[...]

    @pl.when(i < N_PROMPT_TILES)
    def _prompt():
        x = hp_ref[...]
        u = _rms(x, g)
        first = (i % TILES_PER_SEQ) == 0
        halo = jnp.where(first, 0.0, _rms(halo_ref[...], g))
        ext_ref[0:HALO, :] = halo
        ext_ref[HALO:HALO + ROW_TILE, :] = u
        poolp_ref[0] = ext_ref[HALO + ROW_TILE - POOL_STATE:HALO + ROW_TILE, :]
        pos = (i % TILES_PER_SEQ) * ROW_TILE + lax.broadcasted_iota(jnp.int32, (ROW_TILE, 1), 0)
        for gi, w in enumerate(POOL_WINDOWS):
            sl = slice(gi * POOL_GROUP, (gi + 1) * POOL_GROUP)
            ug = u[:, sl]
            s = ug
            for j in range(1, w):
                s = s + ext_ref[HALO - j:HALO - j + ROW_TILE, sl]
            inv = 1.0 / jnp.minimum(pos + 1, w).astype(_F32)
            d = s * inv - ug
            o = _dot(d.astype(_BF16), w_ref[gi])
            out_ref[:, sl] = x[:, sl] + o * sc_ref[:, sl]

    @pl.when(i == N_PROMPT_TILES)
    def _sample():
        x = hs_ref[...]
        u = _rms(x, g)
        u3 = u.reshape(DEC_BATCH, DEC_SEQ, D_MODEL)
        ext3_ref[:, HALO - POOL_STATE:HALO, :] = past_ref[...]
        ext3_ref[:, HALO:HALO + DEC_SEQ, :] = u3
        pools_ref[:, 0:POOL_STATE - DEC_SEQ, :] = past_ref[:, DEC_SEQ:POOL_STATE, :]
        pools_ref[:, POOL_STATE - DEC_SEQ:POOL_STATE, :] = u3
        for gi, w in enumerate(POOL_WINDOWS):
            sl = slice(gi * POOL_GROUP, (gi + 1) * POOL_GROUP)
            ug = u3[:, :, sl]
            s = ug
            for j in range(1, w):
                s = s + ext3_ref[:, HALO - j:HALO - j + DEC_SEQ, sl]
            d = s * (1.0 / w) - ug
            o = _dot(d.reshape(N_SAMPLE, POOL_GROUP).astype(_BF16), w_ref[gi])
            out_ref[:, sl] = x[:, sl] + o * sc_ref[:, sl]


def _pool_layer(layer, h_prompt, h_sample, sample_tile, state_pool, norm_a, w_pool, pool_scale,
                convert=()):
    halo_blocks_per_tile = ROW_TILE // HALO
    c_in, c_args, c_out, c_shape = _convert_specs(convert)
    in_specs = [
        pl.BlockSpec((ROW_TILE, D_MODEL), lambda i: (jnp.minimum(i, N_PROMPT_TILES - 1), 0)),
        pl.BlockSpec((HALO, D_MODEL),
                     lambda i: (jnp.clip(i * halo_blocks_per_tile - 1, 0, N_PROMPT // HALO - 1), 0)),
        pl.BlockSpec((ROW_TILE, D_MODEL), lambda i: (sample_tile, 0)),
        pl.BlockSpec((None, DEC_BATCH, POOL_STATE, D_MODEL), lambda i: (layer, 0, 0, 0),
                     pipeline_mode=pl.Buffered(1)),
        pl.BlockSpec((None, 1, D_MODEL), lambda i: (layer, 0, 0)),
        pl.BlockSpec((None, len(POOL_WINDOWS), POOL_GROUP, POOL_GROUP), lambda i: (layer, 0, 0, 0)),
        pl.BlockSpec((None, 1, D_MODEL), lambda i: (layer, 0, 0)),
    ]
    out_specs = [
        pl.BlockSpec((ROW_TILE, D_MODEL), lambda i: (i, 0)),
        pl.BlockSpec((1, POOL_STATE, D_MODEL),
                     lambda i: (jnp.minimum(i // TILES_PER_SEQ, BATCH - 1), 0, 0)),
        pl.BlockSpec((DEC_BATCH, POOL_STATE, D_MODEL), lambda i: (0, 0, 0)),
    ]
    out_shape = [
        jax.ShapeDtypeStruct((N_ROWS, D_MODEL), _F32),
        jax.ShapeDtypeStruct((BATCH, POOL_STATE, D_MODEL), _F32),
        jax.ShapeDtypeStruct((DEC_BATCH, POOL_STATE, D_MODEL), _F32),
    ]
    return pl.pallas_call(
        _with_convert(_pool_kernel, len(in_specs), len(out_specs), len(convert)),
        grid=(N_ROW_TILES,),
        in_specs=in_specs + c_in,
        out_specs=out_specs + c_out,
        out_shape=out_shape + c_shape,
        scratch_shapes=[
            pltpu.VMEM((HALO + ROW_TILE, D_MODEL), _F32),
            pltpu.VMEM((DEC_BATCH, HALO + DEC_SEQ, D_MODEL), _F32),
        ],
        compiler_params=pltpu.CompilerParams(
            dimension_semantics=("arbitrary",), vmem_limit_bytes=VMEM_LIMIT),
        name="pool_layer",
    )(h_prompt, h_prompt, h_sample, state_pool, norm_a, w_pool, pool_scale, *c_args)


def _mlp_kernel(h_ref, g_ref, wup_hbm, wdn_hbm, *rest, final):
    if final:
        gf_ref, out_ref, outs_ref, hn_ref, wup_buf, wdn_buf, sem = rest
    else:
        out_ref, hn_ref, wup_buf, wdn_buf, sem = rest
    i = pl.program_id(0)

    def weight_copies(f):
        blk = pl.ds((f % N_FF_TILES) * MLP_FF_TILE, MLP_FF_TILE)
        slot = f % 2
        return (pltpu.make_async_copy(wup_hbm.at[:, blk], wup_buf.at[slot], sem.at[0, slot]),
                pltpu.make_async_copy(wdn_hbm.at[blk, :], wdn_buf.at[slot], sem.at[1, slot]))

    @pl.when(i == 0)
    def _prime():
        for c in weight_copies(0):
            c.start()

    x = h_ref[...]
    hn_ref[...] = _rms(x, g_ref[...]).astype(_BF16)
    out_ref[...] = x
    for f in range(N_FF_TILES):
        if f + 1 < N_FF_TILES:
            for c in weight_copies(f + 1):
                c.start()
        else:
            @pl.when(i + 1 < pl.num_programs(0))
            def _next_tile():
                for c in weight_copies(f + 1):
                    c.start()
        for c in weight_copies(f):
            c.wait()
        a = jnp.maximum(_dot(hn_ref[...], wup_buf[f % 2]), 0.0)
        out_ref[...] += _dot((a * a).astype(_BF16), wdn_buf[f % 2])

    if final:
        out_ref[...] = _rms(out_ref[...], gf_ref[...])

        @pl.when(i == N_MLP_TILES - 1)
        def _split():
            outs_ref[...] = out_ref[MLP_SAMPLE_ROW0:, :]


def _mlp_layer(layer, h, norm_mlp, w_up, w_down, norm_f=None):
    final = norm_f is not None
    in_specs = [
        pl.BlockSpec((MLP_ROW_TILE, D_MODEL), lambda i: (i, 0)),
        pl.BlockSpec((None, 1, D_MODEL), lambda i: (layer, 0, 0)),
        pl.BlockSpec(memory_space=pl.ANY),
        pl.BlockSpec(memory_space=pl.ANY),
    ]
    args = [h, norm_mlp, w_up, w_down]
    row_spec = pl.BlockSpec((MLP_ROW_TILE, D_MODEL), lambda i: (i, 0))
    if final:
        in_specs.append(pl.BlockSpec((1, D_MODEL), lambda i: (0, 0)))
        args.append(norm_f)
        out_specs = [row_spec, pl.BlockSpec((N_SAMPLE, D_MODEL), lambda i: (0, 0))]
        out_shape = [jax.ShapeDtypeStruct((N_PROMPT, D_MODEL), _F32),
                     jax.ShapeDtypeStruct((N_SAMPLE, D_MODEL), _F32)]
    else:
        out_specs = row_spec
        out_shape = jax.ShapeDtypeStruct((N_ROWS, D_MODEL), _F32)
    return pl.pallas_call(
        functools.partial(_mlp_kernel, final=final),
        grid=(N_MLP_TILES,),
        in_specs=in_specs,
        out_specs=out_specs,
        out_shape=out_shape,
        scratch_shapes=[
            pltpu.VMEM((MLP_ROW_TILE, D_MODEL), _BF16),
            pltpu.VMEM((2, D_MODEL, MLP_FF_TILE), _BF16),
            pltpu.VMEM((2, MLP_FF_TILE, D_MODEL), _BF16),
            pltpu.SemaphoreType.DMA((2, 2)),
        ],
        compiler_params=pltpu.CompilerParams(
            dimension_semantics=("arbitrary",), vmem_limit_bytes=VMEM_LIMIT),
        name="mlp_layer",
    )(*args)


def _kv_kernel(h_ref, g_ref, w_ref, out_ref):
    hk = _rms(h_ref[...], g_ref[...]).astype(_BF16)
    out_ref[...] = _dot(hk, w_ref[...])


def _kv_proj(h, gain, w_kv, convert=()):
    c_in, c_args, c_out, c_shape = _convert_specs(convert)
    in_specs = [
        pl.BlockSpec((ROW_TILE, D_MODEL), lambda i: (i, 0)),
        pl.BlockSpec((1, D_MODEL), lambda i: (0, 0)),
        pl.BlockSpec((D_MODEL, 2 * KV_DIM), lambda i: (0, 0)),
    ]
    return pl.pallas_call(
        _with_convert(_kv_kernel, len(in_specs), 1, len(convert)),
        grid=(N_ROW_TILES,),
        in_specs=in_specs + c_in,
        out_specs=[pl.BlockSpec((ROW_TILE, 2 * KV_DIM), lambda i: (i, 0))] + c_out,
        out_shape=[jax.ShapeDtypeStruct((N_ROWS, 2 * KV_DIM), _F32)] + c_shape,
        compiler_params=pltpu.CompilerParams(
            dimension_semantics=("arbitrary",), vmem_limit_bytes=VMEM_LIMIT),
        name="kv_proj",
    )(h, gain, w_kv, *c_args)


def _pair_diag_tall(x_full, g):
    keys = x_full.shape[0]
    c, e0 = divmod(g, 2)
    xa = x_full[:, c * 2 * HEAD_DIM:(c + 1) * 2 * HEAD_DIM]
    xr = pltpu.roll(xa, HEAD_DIM, 1)
    low_half = lax.broadcasted_iota(jnp.int32, (keys, 2 * HEAD_DIM), 1) < HEAD_DIM
    top = jnp.where(low_half, xa if e0 == 0 else xr, 0.0)
    bot = jnp.where(low_half, 0.0, xr if e0 == 0 else xa)
    return jnp.concatenate([top, bot], axis=0).astype(_BF16)


def _pair_diag_wide(xt, g):
    xg = xt[g * HEAD_DIM:(g + 1) * HEAD_DIM, :]
    z = jnp.zeros_like(xg)
    return jnp.concatenate([jnp.concatenate([xg, z], axis=1),
                            jnp.concatenate([z, xg], axis=1)], axis=0)


def _rows_scores(q_groups, k_full):
    kt = k_full.T.astype(_BF16)
    scores = []
    for g in range(N_KV_HEADS):
        qg = q_groups[g]
        qs = jnp.concatenate([qg[:, p * 128:(p + 1) * 128] for p in range(4)], axis=0)
        scores.append(_dot(qs.astype(_BF16), _pair_diag_wide(kt, g)))
    return scores


def _rows_softmax(scores, allowed, tq, sink_ref):
    probs = []
    for g in range(N_KV_HEADS):
        s = scores[g]
        p_rows = []
        for p in range(4):
            p_cols = []
            for e in range(2):
                sink = sink_ref[g * 8 + p * 2 + e]
                spe = s[p * tq:(p + 1) * tq, e * KEY_SLOTS:(e + 1) * KEY_SLOTS]
                spe = jnp.where(allowed, spe, -jnp.inf)
                m = jnp.maximum(jnp.max(spe, axis=-1, keepdims=True), sink)
                pe = jnp.exp(spe - m)
                l = jnp.sum(pe, axis=-1, keepdims=True) + jnp.exp(sink - m)
                p_cols.append(pe * (1.0 / l))
            p_rows.append(jnp.concatenate(p_cols, axis=1))
        probs.append(jnp.concatenate(p_rows, axis=0).astype(_BF16))
    return probs


def _rows_values(probs, v_full, tq):
    outs = []
    for g in range(N_KV_HEADS):
        o = _dot(probs[g], _pair_diag_tall(v_full, g))
        outs.append(jnp.concatenate([o[p * tq:(p + 1) * tq, :] for p in range(4)], axis=1))
    return outs


def _attn_sample_kernel(sink_ref, h_ref, kv_ref, ck_ref, cv_ref, g_ref, wq_ref, wo_ref,
                        out_ref, q_ref, o_ref):
    x = h_ref[...]
    hn = _rms(x, g_ref[...]).astype(_BF16)
    q = _dot(hn, wq_ref[...]) * (HEAD_DIM ** -0.5)
    for g in range(N_KV_HEADS):
        q_ref[g] = q[:, g * GROUP_LANES:(g + 1) * GROUP_LANES]

    qi = lax.broadcasted_iota(jnp.int32, (DEC_SEQ, KEY_SLOTS), 0)
    kj = lax.broadcasted_iota(jnp.int32, (DEC_SEQ, KEY_SLOTS), 1)
    allowed = ((kj < WINDOW) & (kj > qi)) | ((kj >= WINDOW) & (kj - WINDOW <= qi))
    pad = jnp.zeros((KEY_SLOTS - WINDOW - DEC_SEQ, KV_DIM), _F32)

    def body(j, carry):
        seqs = []
        for u in range(SAMPLE_UNROLL):
            b = j * SAMPLE_UNROLL + u
            rows = pl.ds(pl.multiple_of(b * DEC_SEQ, DEC_SEQ), DEC_SEQ)
            kb = jnp.concatenate([ck_ref[b], kv_ref[rows, 0:KV_DIM], pad], axis=0)
            scores = _rows_scores([q_ref[g, rows, :] for g in range(N_KV_HEADS)], kb)
            seqs.append((b, rows, scores))
        probs = [_rows_softmax(scores, allowed, DEC_SEQ, sink_ref) for _, _, scores in seqs]
        for (b, rows, _), pm in zip(seqs, probs):
            vb = jnp.concatenate([cv_ref[b], kv_ref[rows, KV_DIM:2 * KV_DIM], pad], axis=0)
            outs = _rows_values(pm, vb, DEC_SEQ)
            for g in range(N_KV_HEADS):
                o_ref[g, rows, :] = outs[g]
        return carry

    lax.fori_loop(0, DEC_BATCH // SAMPLE_UNROLL, body, 0)

    acc = x
    for g in range(N_KV_HEADS):
        acc = acc + _dot(o_ref[g].astype(_BF16), wo_ref[g * GROUP_LANES:(g + 1) * GROUP_LANES, :])
    out_ref[...] = acc


def _attn_sample(layer, h, kv, cache_k, cache_v, norm_b, w_q, w_o, sinks):
    const1 = pl.Buffered(1)
    return pl.pallas_call(
        _attn_sample_kernel,
        grid=(1,),
        in_specs=[
            pl.BlockSpec(memory_space=pltpu.SMEM),
            pl.BlockSpec((ROW_TILE, D_MODEL), lambda i: (N_PROMPT_TILES, 0), pipeline_mode=const1),
            pl.BlockSpec((ROW_TILE, 2 * KV_DIM), lambda i: (N_PROMPT_TILES, 0), pipeline_mode=const1),
            pl.BlockSpec((DEC_BATCH, WINDOW, KV_DIM), lambda i: (0, 0, 0), pipeline_mode=const1),
            pl.BlockSpec((DEC_BATCH, WINDOW, KV_DIM), lambda i: (0, 0, 0), pipeline_mode=const1),
            pl.BlockSpec((None, 1, D_MODEL), lambda i: (layer, 0, 0)),
            pl.BlockSpec((D_MODEL, D_MODEL), lambda i: (0, 0), pipeline_mode=const1),
            pl.BlockSpec((D_MODEL, D_MODEL), lambda i: (0, 0), pipeline_mode=const1),
        ],
        out_specs=pl.BlockSpec((N_SAMPLE, D_MODEL), lambda i: (0, 0)),
        out_shape=jax.ShapeDtypeStruct((N_SAMPLE, D_MODEL), _F32),
        scratch_shapes=[
            pltpu.VMEM((N_KV_HEADS, N_SAMPLE, GROUP_LANES), _F32),
            pltpu.VMEM((N_KV_HEADS, N_SAMPLE, GROUP_LANES), _F32),
        ],
        compiler_params=pltpu.CompilerParams(
            dimension_semantics=("arbitrary",), vmem_limit_bytes=VMEM_LIMIT),
        name="attn_sample",
    )(sinks, h, kv, cache_k, cache_v, norm_b, w_q, w_o)


def _attn_core_lanes(qt_ref, ot_ref, cols, k_full, v_full, allowed_t, sink_ref, between):
    vt = v_full.T.astype(_BF16)

    def scores(g):
        base = g * GROUP_LANES
        qst = jnp.concatenate([qt_ref[base + p * 128:base + (p + 1) * 128, cols]
                               for p in range(4)], axis=1)
        return _dot(_pair_diag_tall(k_full, g), qst)

    def softmax(g, st):
        inv = {}
        pt_rows = []
        for e in range(2):
            pt_cols = []
            for p in range(4):
                sink = sink_ref[g * 8 + p * 2 + e]
                sub = st[e * KEY_SLOTS:(e + 1) * KEY_SLOTS, p * WINDOW:(p + 1) * WINDOW]
                sub = jnp.where(allowed_t, sub, -jnp.inf)
                m = jnp.maximum(jnp.max(sub, axis=0, keepdims=True), sink)
                pe = jnp.exp(sub - m)
                l = jnp.sum(pe, axis=0, keepdims=True) + jnp.exp(sink - m)
                inv[e, p] = 1.0 / l
                pt_cols.append(pe.astype(_BF16))
            pt_rows.append(jnp.concatenate(pt_cols, axis=1))
        return jnp.concatenate(pt_rows, axis=0), inv

    def values(g, pt, inv):
        base = g * GROUP_LANES
        ot = _dot(_pair_diag_wide(vt, g), pt)
        for e in range(2):
            for p in range(4):
                r0 = base + p * 128 + e * HEAD_DIM
                ot_ref[r0:r0 + HEAD_DIM, cols] = (
                    ot[e * HEAD_DIM:(e + 1) * HEAD_DIM, p * WINDOW:(p + 1) * WINDOW] * inv[e, p])

    st = {0: scores(0), 1: scores(1)}
    between()
    for g in range(N_KV_HEADS):
        pt, inv = softmax(g, st.pop(g))
        if g + 2 < N_KV_HEADS:
            st[g + 2] = scores(g + 2)
        values(g, pt, inv)


def _attn_prompt_kernel(sink_ref, h_ref, kv_ref, kvp_ref, hs_ref, g_ref, wq_ref, wo_ref,
                        out_ref, qt_ref, ot_ref):
    i = pl.program_id(0)

    @pl.when(i < N_PROMPT_TILES)
    def _prompt():
        n_blocks = ROW_TILE // WINDOW
        cols = [slice(n * WINDOW, (n + 1) * WINDOW) for n in range(n_blocks)]
        hn = _rms(h_ref[...], g_ref[...]).astype(_BF16)

        def project_q(n):
            q = _dot(hn[cols[n], :], wq_ref[...]) * (HEAD_DIM ** -0.5)
            qt_ref[:, cols[n]] = q.T.astype(_BF16)

        def project_o(n):
            o = ot_ref[:, cols[n]].T.astype(_BF16)
            out_ref[cols[n], :] = h_ref[cols[n], :] + _dot(o, wo_ref[...])

        def between_blocks(n):
            if n > 0:
                project_o(n - 1)
            if n + 1 < n_blocks:
                project_q(n + 1)

        first = (i % TILES_PER_SEQ) == 0
        kj = lax.broadcasted_iota(jnp.int32, (KEY_SLOTS, WINDOW), 0)
        qi = lax.broadcasted_iota(jnp.int32, (KEY_SLOTS, WINDOW), 1)
        band = (kj > qi) & (kj <= qi + WINDOW)
        kv_cat = jnp.concatenate([kvp_ref[...], kv_ref[...]], axis=0)
        project_q(0)
        for n in range(n_blocks):
            allowed_t = band
            if n == 0:
                allowed_t = band & ((kj >= WINDOW) | jnp.logical_not(first))
            kb = kv_cat[n * WINDOW:n * WINDOW + KEY_SLOTS, 0:KV_DIM]
            vb = kv_cat[n * WINDOW:n * WINDOW + KEY_SLOTS, KV_DIM:2 * KV_DIM]
            _attn_core_lanes(qt_ref, ot_ref, cols[n], kb, vb, allowed_t, sink_ref,
                             functools.partial(between_blocks, n))
        project_o(n_blocks - 1)

    @pl.when(i == N_PROMPT_TILES)
    def _sample_rows():
        out_ref[...] = hs_ref[...]


def _attn_layer(layer, h, kv, cache_k, cache_v, norm_b, w_q, w_o, sinks, convert=()):
    h_sample = _attn_sample(layer, h, kv, cache_k, cache_v, norm_b, w_q, w_o, sinks)
    kv_blocks_per_tile = ROW_TILE // WINDOW
    const1 = pl.Buffered(1)
    c_in, c_args, c_out, c_shape = _convert_specs(convert)
    in_specs = [
        pl.BlockSpec(memory_space=pltpu.SMEM),
        pl.BlockSpec((ROW_TILE, D_MODEL), lambda i: (i, 0)),
        pl.BlockSpec((ROW_TILE, 2 * KV_DIM), lambda i: (i, 0)),
        pl.BlockSpec((WINDOW, 2 * KV_DIM), lambda i: (jnp.maximum(i * kv_blocks_per_tile - 1, 0), 0)),
        pl.BlockSpec((N_SAMPLE, D_MODEL), lambda i: (0, 0), pipeline_mode=const1),
        pl.BlockSpec((None, 1, D_MODEL), lambda i: (layer, 0, 0)),
        pl.BlockSpec((D_MODEL, D_MODEL), lambda i: (0, 0), pipeline_mode=const1),
        pl.BlockSpec((D_MODEL, D_MODEL), lambda i: (0, 0), pipeline_mode=const1),
    ]
    return pl.pallas_call(
        _with_convert(_attn_prompt_kernel, len(in_specs), 1, len(convert)),
        grid=(N_ROW_TILES,),
        in_specs=in_specs + c_in,
        out_specs=[pl.BlockSpec((ROW_TILE, D_MODEL), lambda i: (i, 0))] + c_out,
        out_shape=[jax.ShapeDtypeStruct((N_ROWS, D_MODEL), _F32)] + c_shape,
        scratch_shapes=[
            pltpu.VMEM((D_MODEL, ROW_TILE), _BF16),
            pltpu.VMEM((D_MODEL, ROW_TILE), _F32),
        ],
        compiler_params=pltpu.CompilerParams(
            dimension_semantics=("arbitrary",), vmem_limit_bytes=VMEM_LIMIT),
        name="attn_prompt",
    )(sinks, h, kv, kv, h_sample, norm_b, w_q, w_o, *c_args)


def kernel(x_prompt, x_sample, state_pool, cache_k_win, cache_v_win, norm_a, w_pool, pool_scale,
           norm_kv, w_k, w_v, norm_b, w_q, w_o, sinks, norm_mlp, w_up, w_down, norm_f):
    cache_k = cache_k_win.reshape(DEC_BATCH, WINDOW, KV_DIM)
    cache_v = cache_v_win.reshape(DEC_BATCH, WINDOW, KV_DIM)
    rows3 = lambda v: v.reshape(v.shape[0], 1, D_MODEL)
    norm_a3, scale3, norm_b3, norm_mlp3 = rows3(norm_a), rows3(pool_scale), rows3(norm_b), rows3(norm_mlp)
    w_pool_b = w_pool.astype(_BF16)
    w_kv = jnp.concatenate([w_k, w_v], axis=1).astype(_BF16)
    mlp_weights = lambda l: [(w_up, l), (w_down, l)]
    attn_weights = lambda j: [(w_q, j), (w_o, j)]

    pool_p, pool_s = [], []
    h_prompt = x_prompt.reshape(N_PROMPT, D_MODEL)
    h_sample, sample_tile = x_sample.reshape(N_SAMPLE, D_MODEL), 0
    for l in range(N_A_LAYERS):
        h, pp, ps, w_up_b, w_down_b = _pool_layer(
            l, h_prompt, h_sample, sample_tile, state_pool, norm_a3, w_pool_b, scale3,
            convert=mlp_weights(l))
        pool_p.append(pp)
        pool_s.append(ps)
        h = _mlp_layer(l, h, norm_mlp3, w_up_b, w_down_b)
        h_prompt, h_sample, sample_tile = h, h, N_PROMPT_TILES
    kv, w_q_b, w_o_b = _kv_proj(h, norm_kv.reshape(1, D_MODEL), w_kv, convert=attn_weights(0))
    for j in range(N_B_LAYERS):
        l = N_A_LAYERS + j
        convert = mlp_weights(l) + (attn_weights(j + 1) if j + 1 < N_B_LAYERS else [])
        h, w_up_b, w_down_b, *w_attn = _attn_layer(
            j, h, kv, cache_k, cache_v, norm_b3, w_q_b, w_o_b, sinks[j], convert=convert)
        if w_attn:
            w_q_b, w_o_b = w_attn
        if l < DEPTH - 1:
            h = _mlp_layer(l, h, norm_mlp3, w_up_b, w_down_b)
        else:
            y_prompt, y_sample = _mlp_layer(l, h, norm_mlp3, w_up_b, w_down_b,
                                            norm_f=norm_f.reshape(1, D_MODEL))

    k_all, v_all = kv[:, :KV_DIM], kv[:, KV_DIM:]

    def prompt_window(a):
        return a[:N_PROMPT].reshape(BATCH, SEQ, N_KV_HEADS, HEAD_DIM)[:, SEQ - WINDOW:]

    def sample_window(buf, a):
        new = a[N_PROMPT:].reshape(DEC_BATCH, DEC_SEQ, N_KV_HEADS, HEAD_DIM)
        return jnp.concatenate([buf, new], axis=1)[:, DEC_SEQ:]

    return (y_prompt.reshape(BATCH, SEQ, D_MODEL), y_sample.reshape(DEC_BATCH, DEC_SEQ, D_MODEL),
            jnp.stack(pool_p), jnp.stack(pool_s),
            prompt_window(k_all), prompt_window(v_all),
            sample_window(cache_k_win, k_all), sample_window(cache_v_win, v_all))
```

```python
import functools

import jax
import jax.numpy as jnp
import numpy as np
from jax import lax
from jax.experimental import pallas as pl
from jax.experimental.pallas import tpu as pltpu

D_MODEL = 2048
BATCH = 4
SEQ = 2048
DEC_BATCH = 32
DEC_SEQ = 8
N_A_LAYERS = 2
N_B_LAYERS = 2
DEPTH = 4
POOL_WINDOWS = (2, 4, 8, 16)
POOL_GROUP = D_MODEL // len(POOL_WINDOWS)
POOL_STATE = 15
HEAD_DIM = 64
N_HEADS = 32
N_KV_HEADS = 4
KV_DIM = N_KV_HEADS * HEAD_DIM
GROUP_LANES = D_MODEL // N_KV_HEADS
WINDOW = 128
D_FF = 4 * D_MODEL
RMS_EPS = 1e-5

N_PROMPT = BATCH * SEQ
N_SAMPLE = DEC_BATCH * DEC_SEQ
N_ROWS = N_PROMPT + N_SAMPLE

ROW_TILE = 256
N_ROW_TILES = N_ROWS // ROW_TILE
TILES_PER_SEQ = SEQ // ROW_TILE
N_PROMPT_TILES = N_PROMPT // ROW_TILE
HALO = 16
MXU_HALO = 128
MXU_POOL_WINDOWS = (8, 16)
ATT_TILE = 512
N_ATT_TILES = -(-N_ROWS // ATT_TILE)
N_ATT_PROMPT_TILES = N_PROMPT // ATT_TILE
ATT_TILES_PER_SEQ = SEQ // ATT_TILE
MLP_ROW_TILE = 768
MLP_FF_TILE = 1024
N_MLP_TILES = N_ROWS // MLP_ROW_TILE
MLP_SAMPLE_ROW0 = N_PROMPT - (N_MLP_TILES - 1) * MLP_ROW_TILE
assert N_MLP_TILES * MLP_ROW_TILE == N_ROWS and MLP_SAMPLE_ROW0 + N_SAMPLE == MLP_ROW_TILE
KEY_SLOTS = 2 * WINDOW
SAMPLE_UNROLL = 4
CONVERT_STEPS = 64

VMEM_LIMIT = 56 * 1024 * 1024

_BF16 = jnp.bfloat16
_F32 = jnp.float32


def _rms(x, g):
    ms = jnp.mean(x * x, axis=-1, keepdims=True)
    return (x * lax.rsqrt(ms + RMS_EPS)) * g


def _dot(a, b):
    return jnp.dot(a, b, preferred_element_type=_F32)


def _pool_band_matrices():
    t = np.arange(ROW_TILE)[:, None]
    r = np.arange(MXU_HALO + ROW_TILE)[None, :]
    mats = []
    for w in MXU_POOL_WINDOWS:
        band = ((r > t + MXU_HALO - w) & (r <= t + MXU_HALO)).astype(np.float32)
        mats.append(np.concatenate([band, band], axis=1))
    return jnp.asarray(np.stack(mats), dtype=_BF16)


def _pool_kernel(hp_ref, halo_ref, hs_ref, past_ref, g_ref, w_ref, sc_ref, band_ref,
                 out_ref, poolp_ref, pools_ref, ext_ref, ext3_ref):
    i = pl.program_id(0)
    g = g_ref[...]

    @pl.when(i == 0)
    def _zero_unused_lookback():
        ext_ref[0:MXU_HALO - HALO, :] = jnp.zeros((MXU_HALO - HALO, D_MODEL), _F32)

    @pl.when(i < N_PROMPT_TILES)
    def _prompt():
        x = hp_ref[...]
        u = _rms(x, g)
        first = (i % TILES_PER_SEQ) == 0
        halo = jnp.where(first, 0.0, _rms(halo_ref[...], g))
        ext_ref[MXU_HALO - HALO:MXU_HALO, :] = halo
        ext_ref[MXU_HALO:MXU_HALO + ROW_TILE, :] = u
        poolp_ref[0] = ext_ref[MXU_HALO + ROW_TILE - POOL_STATE:MXU_HALO + ROW_TILE, :]
        pos = (i % TILES_PER_SEQ) * ROW_TILE + lax.broadcasted_iota(jnp.int32, (ROW_TILE, 1), 0)
        groups = [slice(gi * POOL_GROUP, (gi + 1) * POOL_GROUP) for gi in range(len(POOL_WINDOWS))]
        mxu_sums = {}
        for k, w in enumerate(MXU_POOL_WINDOWS):
            e = ext_ref[:, groups[POOL_WINDOWS.index(w)]]
            hi = e.astype(_BF16)
            lo = (e - hi.astype(_F32)).astype(_BF16)
            mxu_sums[w] = _dot(band_ref[k], jnp.concatenate([hi, lo], axis=0))
        for gi, w in enumerate(POOL_WINDOWS):
            sl = groups[gi]
            ug = u[:, sl]
            if w in mxu_sums:
                s = mxu_sums[w]
            else:
                s = ug
                for j in range(1, w):
                    s = s + ext_ref[MXU_HALO - j:MXU_HALO - j + ROW_TILE, sl]
            inv = 1.0 / jnp.minimum(pos + 1, w).astype(_F32)
            d = s * inv - ug
            o = _dot(d.astype(_BF16), w_ref[gi])
            out_ref[:, sl] = x[:, sl] + o * sc_ref[:, sl]

    @pl.when(i == N_PROMPT_TILES)
    def _sample():
        x = hs_ref[...]
        u = _rms(x, g)
        u3 = u.reshape(DEC_BATCH, DEC_SEQ, D_MODEL)
        ext3_ref[:, HALO - POOL_STATE:HALO, :] = past_ref[...]
        ext3_ref[:, HALO:HALO + DEC_SEQ, :] = u3
        pools_ref[:, 0:POOL_STATE - DEC_SEQ, :] = past_ref[:, DEC_SEQ:POOL_STATE, :]
        pools_ref[:, POOL_STATE - DEC_SEQ:POOL_STATE, :] = u3
        for gi, w in enumerate(POOL_WINDOWS):
            sl = slice(gi * POOL_GROUP, (gi + 1) * POOL_GROUP)
            ug = u3[:, :, sl]
            s = ug
            for j in range(1, w):
                s = s + ext3_ref[:, HALO - j:HALO - j + DEC_SEQ, sl]
            d = s * (1.0 / w) - ug
            o = _dot(d.reshape(N_SAMPLE, POOL_GROUP).astype(_BF16), w_ref[gi])
            out_ref[:, sl] = x[:, sl] + o * sc_ref[:, sl]


def _pool_layer(layer, h_prompt, h_sample, sample_tile, state_pool, norm_a, w_pool, pool_scale):
    halo_blocks_per_tile = ROW_TILE // HALO
    return pl.pallas_call(
        _pool_kernel,
        grid=(N_ROW_TILES,),
        in_specs=[
            pl.BlockSpec((ROW_TILE, D_MODEL), lambda i: (jnp.minimum(i, N_PROMPT_TILES - 1), 0)),
            pl.BlockSpec((HALO, D_MODEL),
                         lambda i: (jnp.clip(i * halo_blocks_per_tile - 1, 0,
                                             N_PROMPT // HALO - 1), 0)),
            pl.BlockSpec((ROW_TILE, D_MODEL), lambda i: (sample_tile, 0)),
            pl.BlockSpec((None, DEC_BATCH, POOL_STATE, D_MODEL), lambda i: (layer, 0, 0, 0),
                         pipeline_mode=pl.Buffered(1)),
            pl.BlockSpec((None, 1, D_MODEL), lambda i: (layer, 0, 0)),
            pl.BlockSpec((None, len(POOL_WINDOWS), POOL_GROUP, POOL_GROUP),
                         lambda i: (layer, 0, 0, 0)),
            pl.BlockSpec((None, 1, D_MODEL), lambda i: (layer, 0, 0)),
            pl.BlockSpec((len(MXU_POOL_WINDOWS), ROW_TILE, 2 * (MXU_HALO + ROW_TILE)),
                         lambda i: (0, 0, 0)),
        ],
        out_specs=[
            pl.BlockSpec((ROW_TILE, D_MODEL), lambda i: (i, 0)),
            pl.BlockSpec((1, POOL_STATE, D_MODEL),
                         lambda i: (jnp.minimum(i // TILES_PER_SEQ, BATCH - 1), 0, 0)),
            pl.BlockSpec((DEC_BATCH, POOL_STATE, D_MODEL), lambda i: (0, 0, 0)),
        ],
        out_shape=[
            jax.ShapeDtypeStruct((N_ROWS, D_MODEL), _F32),
            jax.ShapeDtypeStruct((BATCH, POOL_STATE, D_MODEL), _F32),
            jax.ShapeDtypeStruct((DEC_BATCH, POOL_STATE, D_MODEL), _F32),
        ],
        scratch_shapes=[
            pltpu.VMEM((MXU_HALO + ROW_TILE, D_MODEL), _F32),
            pltpu.VMEM((DEC_BATCH, HALO + DEC_SEQ, D_MODEL), _F32),
        ],
        compiler_params=pltpu.CompilerParams(
            dimension_semantics=("arbitrary",), vmem_limit_bytes=VMEM_LIMIT),
        name="pool_layer",
    )(h_prompt, h_prompt, h_sample, state_pool, norm_a, w_pool, pool_scale, _pool_band_matrices())


def _mlp_kernel(*refs, final, n_convert):
    h_ref, g_ref, wup_ref, wdn_ref = refs[:4]
    n_in = 4 + int(final) + n_convert
    gf_ref = refs[4] if final else None
    convert_src = refs[n_in - n_convert:n_in]
    out_ref = refs[n_in]
    outs_ref = refs[n_in + 1] if final else None
    n_out = 1 + int(final)
    convert_dst = refs[n_in + n_out:n_in + n_out + n_convert]
    hn_ref = refs[n_in + n_out + n_convert]
    i = pl.program_id(0)
    f = pl.program_id(1)

    @pl.when(f == 0)
    def _init():
        x = h_ref[...]
        hn_ref[...] = _rms(x, g_ref[...]).astype(_BF16)
        out_ref[...] = x

    a = jnp.maximum(_dot(hn_ref[...], wup_ref[...]), 0.0)
    out_ref[...] += _dot((a * a).astype(_BF16), wdn_ref[...])

    if n_convert:
        @pl.when(i * pl.num_programs(1) + f < CONVERT_STEPS)
        def _convert():
            for src, dst in zip(convert_src, convert_dst):
                dst[...] = src[...].astype(_BF16)

    if final:
        last_f = f == pl.num_programs(1) - 1

        @pl.when(last_f)
        def _finish():
            out_ref[...] = _rms(out_ref[...], gf_ref[...])

        @pl.when(last_f & (i == N_MLP_TILES - 1))
        def _split():
            outs_ref[...] = out_ref[MLP_SAMPLE_ROW0:, :]


def _mlp_layer(layer, h, norm_mlp, w_up, w_down, norm_f=None, convert=()):
    final = norm_f is not None
    n_ff = D_FF // MLP_FF_TILE
    assert N_MLP_TILES * n_ff >= CONVERT_STEPS
    in_specs = [
        pl.BlockSpec((MLP_ROW_TILE, D_MODEL), lambda i, f: (i, 0)),
        pl.BlockSpec((None, 1, D_MODEL), lambda i, f: (layer, 0, 0)),
        pl.BlockSpec((D_MODEL, MLP_FF_TILE), lambda i, f: (0, f)),
        pl.BlockSpec((MLP_FF_TILE, D_MODEL), lambda i, f: (f, 0)),
    ]
    args = [h, norm_mlp, w_up, w_down]
    row_spec = pl.BlockSpec((MLP_ROW_TILE, D_MODEL), lambda i, f: (i, 0))
    if final:
        in_specs.append(pl.BlockSpec((1, D_MODEL), lambda i, f: (0, 0)))
        args.append(norm_f)
        out_specs = [row_spec, pl.BlockSpec((N_SAMPLE, D_MODEL), lambda i, f: (0, 0))]
        out_shape = [jax.ShapeDtypeStruct((N_PROMPT, D_MODEL), _F32),
                     jax.ShapeDtypeStruct((N_SAMPLE, D_MODEL), _F32)]
    else:
        out_specs = [row_spec]
        out_shape = [jax.ShapeDtypeStruct((N_ROWS, D_MODEL), _F32)]
    band = lambda i, f: jnp.minimum(i * n_ff + f, CONVERT_STEPS - 1)
    for stacked, idx in convert:
        _, rows, cols = stacked.shape
        rb = rows // CONVERT_STEPS
        assert rb * CONVERT_STEPS == rows and rb % 16 == 0
        in_specs.append(pl.BlockSpec((None, rb, cols), lambda i, f, idx=idx: (idx, band(i, f), 0)))
        out_specs.append(pl.BlockSpec((rb, cols), lambda i, f: (band(i, f), 0)))
        out_shape.append(jax.ShapeDtypeStruct((rows, cols), _BF16))
        args.append(stacked)
    return pl.pallas_call(
        functools.partial(_mlp_kernel, final=final, n_convert=len(convert)),
        grid=(N_MLP_TILES, n_ff),
        in_specs=in_specs,
        out_specs=out_specs,
        out_shape=out_shape,
        scratch_shapes=[pltpu.VMEM((MLP_ROW_TILE, D_MODEL), _BF16)],
        compiler_params=pltpu.CompilerParams(
            dimension_semantics=("arbitrary", "arbitrary"), vmem_limit_bytes=VMEM_LIMIT),
        name="mlp_layer",
    )(*args)


def _kv_kernel(h_ref, g_ref, w_ref, out_ref):
    hk = _rms(h_ref[...], g_ref[...]).astype(_BF16)
    out_ref[...] = _dot(hk, w_ref[...])


def _kv_proj(h, gain, w_kv):
    return pl.pallas_call(
        _kv_kernel,
        grid=(N_MLP_TILES,),
        in_specs=[
            pl.BlockSpec((MLP_ROW_TILE, D_MODEL), lambda i: (i, 0)),
            pl.BlockSpec((1, D_MODEL), lambda i: (0, 0)),
            pl.BlockSpec((D_MODEL, 2 * KV_DIM), lambda i: (0, 0)),
        ],
        out_specs=pl.BlockSpec((MLP_ROW_TILE, 2 * KV_DIM), lambda i: (i, 0)),
        out_shape=jax.ShapeDtypeStruct((N_ROWS, 2 * KV_DIM), _F32),
        compiler_params=pltpu.CompilerParams(
            dimension_semantics=("arbitrary",), vmem_limit_bytes=VMEM_LIMIT),
        name="kv_proj",
    )(h, gain, w_kv)


def _pair_diag_tall(x_full, g):
    keys = x_full.shape[0]
    c, e0 = divmod(g, 2)
    xa = x_full[:, c * 2 * HEAD_DIM:(c + 1) * 2 * HEAD_DIM]
    xr = pltpu.roll(xa, HEAD_DIM, 1)
    low_half = lax.broadcasted_iota(jnp.int32, (keys, 2 * HEAD_DIM), 1) < HEAD_DIM
    top = jnp.where(low_half, xa if e0 == 0 else xr, 0.0)
    bot = jnp.where(low_half, 0.0, xr if e0 == 0 else xa)
    return jnp.concatenate([top, bot], axis=0).astype(_BF16)


def _pair_diag_wide(xt, g):
    xg = xt[g * HEAD_DIM:(g + 1) * HEAD_DIM, :]
    z = jnp.zeros_like(xg)
    return jnp.concatenate([jnp.concatenate([xg, z], axis=1),
                            jnp.concatenate([z, xg], axis=1)], axis=0)


def _rows_scores(q_groups, k_full):
    kt = k_full.T.astype(_BF16)
    scores = []
    for g in range(N_KV_HEADS):
        qg = q_groups[g]
        qs = jnp.concatenate([qg[:, p * 128:(p + 1) * 128] for p in range(4)], axis=0)
        scores.append(_dot(qs.astype(_BF16), _pair_diag_wide(kt, g)))
    return scores


def _rows_softmax(scores, allowed, tq, sink_ref):
    probs = []
    for g in range(N_KV_HEADS):
        s = scores[g]
        p_rows = []
        for p in range(4):
            p_cols = []
            for e in range(2):
                sink = sink_ref[g * 8 + p * 2 + e]
                spe = s[p * tq:(p + 1) * tq, e * KEY_SLOTS:(e + 1) * KEY_SLOTS]
                spe = jnp.where(allowed, spe, -jnp.inf)
                m = jnp.maximum(jnp.max(spe, axis=-1, keepdims=True), sink)
                pe = jnp.exp(spe - m)
                l = jnp.sum(pe, axis=-1, keepdims=True) + jnp.exp(sink - m)
                p_cols.append(pe * (1.0 / l))
            p_rows.append(jnp.concatenate(p_cols, axis=1))
        probs.append(jnp.concatenate(p_rows, axis=0).astype(_BF16))
    return probs


def _rows_values(probs, v_full, tq):
    outs = []
    for g in range(N_KV_HEADS):
        o = _dot(probs[g], _pair_diag_tall(v_full, g))
        outs.append(jnp.concatenate([o[p * tq:(p + 1) * tq, :] for p in range(4)], axis=1))
    return outs


def _attn_sample_kernel(sink_ref, h_ref, kv_ref, ck_ref, cv_ref, g_ref, wq_ref, wo_ref,
                        out_ref, q_ref, o_ref):
    x = h_ref[...]
    hn = _rms(x, g_ref[...]).astype(_BF16)
    q = _dot(hn, wq_ref[...]) * (HEAD_DIM ** -0.5)
    for g in range(N_KV_HEADS):
        q_ref[g] = q[:, g * GROUP_LANES:(g + 1) * GROUP_LANES]

    qi = lax.broadcasted_iota(jnp.int32, (DEC_SEQ, KEY_SLOTS), 0)
    kj = lax.broadcasted_iota(jnp.int32, (DEC_SEQ, KEY_SLOTS), 1)
    allowed = ((kj < WINDOW) & (kj > qi)) | ((kj >= WINDOW) & (kj - WINDOW <= qi))
    pad = jnp.zeros((KEY_SLOTS - WINDOW - DEC_SEQ, KV_DIM), _F32)

    def body(j, carry):
        seqs = []
        for u in range(SAMPLE_UNROLL):
            b = j * SAMPLE_UNROLL + u
            rows = pl.ds(pl.multiple_of(b * DEC_SEQ, DEC_SEQ), DEC_SEQ)
            kb = jnp.concatenate([ck_ref[b], kv_ref[rows, 0:KV_DIM], pad], axis=0)
            scores = _rows_scores([q_ref[g, rows, :] for g in range(N_KV_HEADS)], kb)
            seqs.append((b, rows, scores))
        probs = [_rows_softmax(scores, allowed, DEC_SEQ, sink_ref) for _, _, scores in seqs]
        for (b, rows, _), pm in zip(seqs, probs):
            vb = jnp.concatenate([cv_ref[b], kv_ref[rows, KV_DIM:2 * KV_DIM], pad], axis=0)
            outs = _rows_values(pm, vb, DEC_SEQ)
            for g in range(N_KV_HEADS):
                o_ref[g, rows, :] = outs[g]
        return carry

    lax.fori_loop(0, DEC_BATCH // SAMPLE_UNROLL, body, 0)

    acc = x
    for g in range(N_KV_HEADS):
        acc = acc + _dot(o_ref[g].astype(_BF16), wo_ref[g * GROUP_LANES:(g + 1) * GROUP_LANES, :])
    out_ref[...] = acc


def _attn_sample(layer, h, kv, cache_k, cache_v, norm_b, w_q, w_o, sinks):
    const1 = pl.Buffered(1)
    sample_block = N_PROMPT // N_SAMPLE
    return pl.pallas_call(
        _attn_sample_kernel,
        grid=(1,),
        in_specs=[
            pl.BlockSpec(memory_space=pltpu.SMEM),
            pl.BlockSpec((N_SAMPLE, D_MODEL), lambda i: (sample_block, 0), pipeline_mode=const1),
            pl.BlockSpec((N_SAMPLE, 2 * KV_DIM), lambda i: (sample_block, 0), pipeline_mode=const1),
            pl.BlockSpec((DEC_BATCH, WINDOW, KV_DIM), lambda i: (0, 0, 0), pipeline_mode=const1),
            pl.BlockSpec((DEC_BATCH, WINDOW, KV_DIM), lambda i: (0, 0, 0), pipeline_mode=const1),
            pl.BlockSpec((None, 1, D_MODEL), lambda i: (layer, 0, 0)),
            pl.BlockSpec((D_MODEL, D_MODEL), lambda i: (0, 0), pipeline_mode=const1),
            pl.BlockSpec((D_MODEL, D_MODEL), lambda i: (0, 0), pipeline_mode=const1),
        ],
        out_specs=pl.BlockSpec((N_SAMPLE, D_MODEL), lambda i: (0, 0)),
        out_shape=jax.ShapeDtypeStruct((N_SAMPLE, D_MODEL), _F32),
        scratch_shapes=[
            pltpu.VMEM((N_KV_HEADS, N_SAMPLE, GROUP_LANES), _F32),
            pltpu.VMEM((N_KV_HEADS, N_SAMPLE, GROUP_LANES), _F32),
        ],
        compiler_params=pltpu.CompilerParams(
            dimension_semantics=("arbitrary",), vmem_limit_bytes=VMEM_LIMIT),
        name="attn_sample",
    )(sinks, h, kv, cache_k, cache_v, norm_b, w_q, w_o)


def _attn_core_lanes(qt_ref, ot_ref, cols, k_full, v_full, allowed_t, sink_ref, between):
    vt = v_full.T.astype(_BF16)

    def scores(g):
        base = g * GROUP_LANES
        qst = jnp.concatenate([qt_ref[base + p * 128:base + (p + 1) * 128, cols]
                               for p in range(4)], axis=1)
        return _dot(_pair_diag_tall(k_full, g), qst)

    def softmax(g, st):
        inv = {}
        pt_rows = []
        for e in range(2):
            pt_cols = []
            for p in range(4):
                sink = sink_ref[g * 8 + p * 2 + e]
                sub = st[e * KEY_SLOTS:(e + 1) * KEY_SLOTS, p * WINDOW:(p + 1) * WINDOW]
                sub = jnp.where(allowed_t, sub, -jnp.inf)
                m = jnp.maximum(jnp.max(sub, axis=0, keepdims=True), sink)
                pe = jnp.exp(sub - m)
                l = jnp.sum(pe, axis=0, keepdims=True) + jnp.exp(sink - m)
                inv[e, p] = 1.0 / l
                pt_cols.append(pe.astype(_BF16))
            pt_rows.append(jnp.concatenate(pt_cols, axis=1))
        return jnp.concatenate(pt_rows, axis=0), inv

    def values(g, pt, inv):
        base = g * GROUP_LANES
        ot = _dot(_pair_diag_wide(vt, g), pt)
        for e in range(2):
            for p in range(4):
                r0 = base + p * 128 + e * HEAD_DIM
                ot_ref[r0:r0 + HEAD_DIM, cols] = (
                    ot[e * HEAD_DIM:(e + 1) * HEAD_DIM, p * WINDOW:(p + 1) * WINDOW] * inv[e, p])

    st = {0: scores(0), 1: scores(1)}
    between()
    for g in range(N_KV_HEADS):
        pt, inv = softmax(g, st.pop(g))
        if g + 2 < N_KV_HEADS:
            st[g + 2] = scores(g + 2)
        values(g, pt, inv)


def _attn_prompt_kernel(sink_ref, h_ref, kv_ref, kvp_ref, hs_ref, g_ref, wq_ref, wo_ref,
                        out_ref, qt_ref, ot_ref):
    i = pl.program_id(0)

    @pl.when(i < N_ATT_PROMPT_TILES)
    def _prompt():
        n_blocks = ATT_TILE // WINDOW
        cols = [slice(n * WINDOW, (n + 1) * WINDOW) for n in range(n_blocks)]

        def project_q(n):
            hn = _rms(h_ref[cols[n], :], g_ref[...]).astype(_BF16)
            q = _dot(hn, wq_ref[...]) * (HEAD_DIM ** -0.5)
            qt_ref[:, cols[n]] = q.T.astype(_BF16)

        def project_o(n):
            o = ot_ref[:, cols[n]].T.astype(_BF16)
            out_ref[cols[n], :] = h_ref[cols[n], :] + _dot(o, wo_ref[...])

        def between_blocks(n):
            if n > 0:
                project_o(n - 1)
            if n + 1 < n_blocks:
                project_q(n + 1)

        first = (i % ATT_TILES_PER_SEQ) == 0
        kj = lax.broadcasted_iota(jnp.int32, (KEY_SLOTS, WINDOW), 0)
        qi = lax.broadcasted_iota(jnp.int32, (KEY_SLOTS, WINDOW), 1)
        band = (kj > qi) & (kj <= qi + WINDOW)
        project_q(0)
        for n in range(n_blocks):
            allowed_t = band
            if n == 0:
                allowed_t = band & ((kj >= WINDOW) | jnp.logical_not(first))
                kv_prev = kvp_ref[...]
            else:
                kv_prev = kv_ref[(n - 1) * WINDOW:n * WINDOW, :]
            kv_blk = jnp.concatenate([kv_prev, kv_ref[cols[n], :]], axis=0)
            _attn_core_lanes(qt_ref, ot_ref, cols[n], kv_blk[:, 0:KV_DIM], kv_blk[:, KV_DIM:],
                             allowed_t, sink_ref, functools.partial(between_blocks, n))
        project_o(n_blocks - 1)

    @pl.when(i == N_ATT_PROMPT_TILES)
    def _sample_rows():
        out_ref[0:N_SAMPLE, :] = hs_ref[...]
        out_ref[N_SAMPLE:, :] = jnp.zeros((ATT_TILE - N_SAMPLE, D_MODEL), _F32)


def _attn_layer(layer, h, kv, cache_k, cache_v, norm_b, w_q, w_o, sinks):
    h_sample = _attn_sample(layer, h, kv, cache_k, cache_v, norm_b, w_q, w_o, sinks)
    kv_blocks_per_tile = ATT_TILE // WINDOW
    const1 = pl.Buffered(1)
    return pl.pallas_call(
        _attn_prompt_kernel,
        grid=(N_ATT_TILES,),
        in_specs=[
            pl.BlockSpec(memory_space=pltpu.SMEM),
            pl.BlockSpec((ATT_TILE, D_MODEL), lambda i: (i, 0)),
            pl.BlockSpec((ATT_TILE, 2 * KV_DIM), lambda i: (i, 0)),
            pl.BlockSpec((WINDOW, 2 * KV_DIM),
                         lambda i: (jnp.maximum(i * kv_blocks_per_tile - 1, 0), 0)),
            pl.BlockSpec((N_SAMPLE, D_MODEL), lambda i: (0, 0), pipeline_mode=const1),
            pl.BlockSpec((None, 1, D_MODEL), lambda i: (layer, 0, 0)),
            pl.BlockSpec((D_MODEL, D_MODEL), lambda i: (0, 0), pipeline_mode=const1),
            pl.BlockSpec((D_MODEL, D_MODEL), lambda i: (0, 0), pipeline_mode=const1),
        ],
        out_specs=pl.BlockSpec((ATT_TILE, D_MODEL), lambda i: (i, 0)),
        out_shape=jax.ShapeDtypeStruct((N_ROWS, D_MODEL), _F32),
        scratch_shapes=[
            pltpu.VMEM((D_MODEL, ATT_TILE), _BF16),
            pltpu.VMEM((D_MODEL, ATT_TILE), _F32),
        ],
        compiler_params=pltpu.CompilerParams(
            dimension_semantics=("arbitrary",), vmem_limit_bytes=VMEM_LIMIT),
        name="attn_prompt",
    )(sinks, h, kv, kv, h_sample, norm_b, w_q, w_o)


def kernel(x_prompt, x_sample, state_pool, cache_k_win, cache_v_win, norm_a, w_pool, pool_scale,
           norm_kv, w_k, w_v, norm_b, w_q, w_o, sinks, norm_mlp, w_up, w_down, norm_f):
    cache_k = cache_k_win.reshape(DEC_BATCH, WINDOW, KV_DIM)
    cache_v = cache_v_win.reshape(DEC_BATCH, WINDOW, KV_DIM)
    rows3 = lambda v: v.reshape(v.shape[0], 1, D_MODEL)
    norm_a3, scale3, norm_b3, norm_mlp3 = rows3(norm_a), rows3(pool_scale), rows3(norm_b), rows3(norm_mlp)
    w_pool_b = w_pool.astype(_BF16)
    w_kv = jnp.concatenate([w_k, w_v], axis=1).astype(_BF16)
    w_up_b, w_down_b = w_up[0].astype(_BF16), w_down[0].astype(_BF16)
    w_q_b = w_o_b = kv = None

    pool_p, pool_s = [], []
    h_prompt = x_prompt.reshape(N_PROMPT, D_MODEL)
    h_sample, sample_tile = x_sample.reshape(N_SAMPLE, D_MODEL), 0
    for l in range(DEPTH):
        if l < N_A_LAYERS:
            h, pp, ps = _pool_layer(l, h_prompt, h_sample, sample_tile, state_pool, norm_a3,
                                    w_pool_b, scale3)
            pool_p.append(pp)
            pool_s.append(ps)
        else:
            h = _attn_layer(l - N_A_LAYERS, h, kv, cache_k, cache_v, norm_b3, w_q_b, w_o_b,
                            sinks[l - N_A_LAYERS])
        if l == DEPTH - 1:
            y_prompt, y_sample = _mlp_layer(l, h, norm_mlp3, w_up_b, w_down_b,
                                            norm_f=norm_f.reshape(1, D_MODEL))
            break
        convert = [(w_up, l + 1), (w_down, l + 1)]
        if l + 1 >= N_A_LAYERS:
            convert += [(w_q, l + 1 - N_A_LAYERS), (w_o, l + 1 - N_A_LAYERS)]
        h, w_up_b, w_down_b, *w_attn = _mlp_layer(l, h, norm_mlp3, w_up_b, w_down_b, convert=convert)
        if w_attn:
            w_q_b, w_o_b = w_attn
        h_prompt, h_sample, sample_tile = h, h, N_PROMPT_TILES
        if l == N_A_LAYERS - 1:
            kv = _kv_proj(h, norm_kv.reshape(1, D_MODEL), w_kv)

    k_all, v_all = kv[:, :KV_DIM], kv[:, KV_DIM:]

    def prompt_window(a):
        return a[:N_PROMPT].reshape(BATCH, SEQ, N_KV_HEADS, HEAD_DIM)[:, SEQ - WINDOW:]

    def sample_window(buf, a):
        new = a[N_PROMPT:].reshape(DEC_BATCH, DEC_SEQ, N_KV_HEADS, HEAD_DIM)
        return jnp.concatenate([buf, new], axis=1)[:, DEC_SEQ:]

    return (y_prompt.reshape(BATCH, SEQ, D_MODEL), y_sample.reshape(DEC_BATCH, DEC_SEQ, D_MODEL),
            jnp.stack(pool_p), jnp.stack(pool_s),
            prompt_window(k_all), prompt_window(v_all),
            sample_window(cache_k_win, k_all), sample_window(cache_v_win, v_all))
```

```python
import functools

import jax
import jax.numpy as jnp
import numpy as np
from jax import lax
from jax.experimental import pallas as pl
from jax.experimental.pallas import tpu as pltpu

D_MODEL = 2048
BATCH = 4
SEQ = 2048
DEC_BATCH = 32
DEC_SEQ = 8
N_A_LAYERS = 2
N_B_LAYERS = 2
DEPTH = 4
POOL_WINDOWS = (2, 4, 8, 16)
POOL_GROUP = D_MODEL // len(POOL_WINDOWS)
POOL_STATE = 15
HEAD_DIM = 64
N_HEADS = 32
N_KV_HEADS = 4
KV_DIM = N_KV_HEADS * HEAD_DIM
GROUP_LANES = D_MODEL // N_KV_HEADS
WINDOW = 128
D_FF = 4 * D_MODEL
RMS_EPS = 1e-5

N_PROMPT = BATCH * SEQ
N_SAMPLE = DEC_BATCH * DEC_SEQ
N_ROWS = N_PROMPT + N_SAMPLE

ROW_TILE = 512
N_ROW_TILES = -(-N_ROWS // ROW_TILE)
TILES_PER_SEQ = SEQ // ROW_TILE
N_PROMPT_TILES = N_PROMPT // ROW_TILE
SAMPLE_BLOCK = N_PROMPT // N_SAMPLE
HALO = 16
MXU_HALO = 128
MXU_POOL_WINDOWS = (8, 16)
MLP_ROW_TILE = 768
MLP_FF_TILE = 1024
N_MLP_TILES = N_ROWS // MLP_ROW_TILE
MLP_SAMPLE_ROW0 = N_PROMPT - (N_MLP_TILES - 1) * MLP_ROW_TILE
assert N_MLP_TILES * MLP_ROW_TILE == N_ROWS and MLP_SAMPLE_ROW0 + N_SAMPLE == MLP_ROW_TILE
KEY_SLOTS = 2 * WINDOW
SAMPLE_UNROLL = 4
CONVERT_STEPS = 64

VMEM_LIMIT = 56 * 1024 * 1024

_BF16 = jnp.bfloat16
_F32 = jnp.float32


def _rms(x, g):
    ms = jnp.mean(x * x, axis=-1, keepdims=True)
    return (x * lax.rsqrt(ms + RMS_EPS)) * g


def _dot(a, b):
    return jnp.dot(a, b, preferred_element_type=_F32)


def _pool_band_matrices():
    t = np.arange(ROW_TILE)[:, None]
    r = np.arange(MXU_HALO + ROW_TILE)[None, :]
    mats = []
    for w in MXU_POOL_WINDOWS:
        band = ((r > t + MXU_HALO - w) & (r <= t + MXU_HALO)).astype(np.float32)
        mats.append(np.concatenate([band, band], axis=1))
    return jnp.asarray(np.stack(mats), dtype=_BF16)


def _pool_kernel(hp_ref, halo_ref, hs_ref, past_ref, g_ref, w_ref, sc_ref, band_ref,
                 out_ref, poolp_ref, pools_ref, ext_ref, ext3_ref):
    i = pl.program_id(0)
    g = g_ref[...]

    @pl.when(i == 0)
    def _zero_unused_lookback():
        ext_ref[0:MXU_HALO - HALO, :] = jnp.zeros((MXU_HALO - HALO, D_MODEL), _F32)

    @pl.when(i < N_PROMPT_TILES)
    def _prompt():
        x = hp_ref[...]
        u = _rms(x, g)
        first = (i % TILES_PER_SEQ) == 0
        halo = jnp.where(first, 0.0, _rms(halo_ref[...], g))
        ext_ref[MXU_HALO - HALO:MXU_HALO, :] = halo
        ext_ref[MXU_HALO:MXU_HALO + ROW_TILE, :] = u
        poolp_ref[0] = ext_ref[MXU_HALO + ROW_TILE - POOL_STATE:MXU_HALO + ROW_TILE, :]
        pos = (i % TILES_PER_SEQ) * ROW_TILE + lax.broadcasted_iota(jnp.int32, (ROW_TILE, 1), 0)
        groups = [slice(gi * POOL_GROUP, (gi + 1) * POOL_GROUP) for gi in range(len(POOL_WINDOWS))]
        mxu_sums = {}
        for k, w in enumerate(MXU_POOL_WINDOWS):
            e = ext_ref[:, groups[POOL_WINDOWS.index(w)]]
            hi = e.astype(_BF16)
            lo = (e - hi.astype(_F32)).astype(_BF16)
            mxu_sums[w] = _dot(band_ref[k], jnp.concatenate([hi, lo], axis=0))
        for gi, w in enumerate(POOL_WINDOWS):
            sl = groups[gi]
            ug = u[:, sl]
            if w in mxu_sums:
                s = mxu_sums[w]
            else:
                s = ug
                for j in range(1, w):
                    s = s + ext_ref[MXU_HALO - j:MXU_HALO - j + ROW_TILE, sl]
            inv = 1.0 / jnp.minimum(pos + 1, w).astype(_F32)
            d = s * inv - ug
            o = _dot(d.astype(_BF16), w_ref[gi])
            out_ref[:, sl] = x[:, sl] + o * sc_ref[:, sl]

    @pl.when(i == N_PROMPT_TILES)
    def _sample():
        x = hs_ref[...]
        u = _rms(x, g)
        u3 = u.reshape(DEC_BATCH, DEC_SEQ, D_MODEL)
        ext3_ref[:, HALO - POOL_STATE:HALO, :] = past_ref[...]
        ext3_ref[:, HALO:HALO + DEC_SEQ, :] = u3
        pools_ref[:, 0:POOL_STATE - DEC_SEQ, :] = past_ref[:, DEC_SEQ:POOL_STATE, :]
        pools_ref[:, POOL_STATE - DEC_SEQ:POOL_STATE, :] = u3
        for gi, w in enumerate(POOL_WINDOWS):
            sl = slice(gi * POOL_GROUP, (gi + 1) * POOL_GROUP)
            ug = u3[:, :, sl]
            s = ug
            for j in range(1, w):
                s = s + ext3_ref[:, HALO - j:HALO - j + DEC_SEQ, sl]
            d = s * (1.0 / w) - ug
            o = _dot(d.reshape(N_SAMPLE, POOL_GROUP).astype(_BF16), w_ref[gi])
            out_ref[0:N_SAMPLE, sl] = x[:, sl] + o * sc_ref[:, sl]
        out_ref[N_SAMPLE:, :] = jnp.zeros((ROW_TILE - N_SAMPLE, D_MODEL), _F32)


def _pool_layer(layer, h_prompt, h_sample, sample_block, state_pool, norm_a, w_pool, pool_scale):
    halo_blocks_per_tile = ROW_TILE // HALO
    return pl.pallas_call(
        _pool_kernel,
        grid=(N_ROW_TILES,),
        in_specs=[
            pl.BlockSpec((ROW_TILE, D_MODEL), lambda i: (jnp.minimum(i, N_PROMPT_TILES - 1), 0)),
            pl.BlockSpec((HALO, D_MODEL),
                         lambda i: (jnp.clip(i * halo_blocks_per_tile - 1, 0,
                                             N_PROMPT // HALO - 1), 0)),
            pl.BlockSpec((N_SAMPLE, D_MODEL), lambda i: (sample_block, 0)),
            pl.BlockSpec((None, DEC_BATCH, POOL_STATE, D_MODEL), lambda i: (layer, 0, 0, 0),
                         pipeline_mode=pl.Buffered(1)),
            pl.BlockSpec((None, 1, D_MODEL), lambda i: (layer, 0, 0)),
            pl.BlockSpec((None, len(POOL_WINDOWS), POOL_GROUP, POOL_GROUP),
                         lambda i: (layer, 0, 0, 0)),
            pl.BlockSpec((None, 1, D_MODEL), lambda i: (layer, 0, 0)),
            pl.BlockSpec((len(MXU_POOL_WINDOWS), ROW_TILE, 2 * (MXU_HALO + ROW_TILE)),
                         lambda i: (0, 0, 0)),
        ],
        out_specs=[
            pl.BlockSpec((ROW_TILE, D_MODEL), lambda i: (i, 0)),
            pl.BlockSpec((1, POOL_STATE, D_MODEL),
                         lambda i: (jnp.minimum(i // TILES_PER_SEQ, BATCH - 1), 0, 0)),
            pl.BlockSpec((DEC_BATCH, POOL_STATE, D_MODEL), lambda i: (0, 0, 0)),
        ],
        out_shape=[
            jax.ShapeDtypeStruct((N_ROWS, D_MODEL), _F32),
            jax.ShapeDtypeStruct((BATCH, POOL_STATE, D_MODEL), _F32),
            jax.ShapeDtypeStruct((DEC_BATCH, POOL_STATE, D_MODEL), _F32),
        ],
        scratch_shapes=[
            pltpu.VMEM((MXU_HALO + ROW_TILE, D_MODEL), _F32),
            pltpu.VMEM((DEC_BATCH, HALO + DEC_SEQ, D_MODEL), _F32),
        ],
        compiler_params=pltpu.CompilerParams(
            dimension_semantics=("arbitrary",), vmem_limit_bytes=VMEM_LIMIT),
        name="pool_layer",
    )(h_prompt, h_prompt, h_sample, state_pool, norm_a, w_pool, pool_scale, _pool_band_matrices())


def _mlp_kernel(*refs, final, n_convert):
    h_ref, g_ref, wup_ref, wdn_ref = refs[:4]
    n_in = 4 + int(final) + n_convert
    gf_ref = refs[4] if final else None
    convert_src = refs[n_in - n_convert:n_in]
    out_ref = refs[n_in]
    outs_ref = refs[n_in + 1] if final else None
    n_out = 1 + int(final)
    convert_dst = refs[n_in + n_out:n_in + n_out + n_convert]
    hn_ref = refs[n_in + n_out + n_convert]
    i = pl.program_id(0)
    f = pl.program_id(1)

    @pl.when(f == 0)
    def _init():
        x = h_ref[...]
        hn_ref[...] = _rms(x, g_ref[...]).astype(_BF16)
        out_ref[...] = x

    a = jnp.maximum(_dot(hn_ref[...], wup_ref[...]), 0.0)
    out_ref[...] += _dot((a * a).astype(_BF16), wdn_ref[...])

    if n_convert:
        @pl.when(i * pl.num_programs(1) + f < CONVERT_STEPS)
        def _convert():
            for src, dst in zip(convert_src, convert_dst):
                dst[...] = src[...].astype(_BF16)

    if final:
        last_f = f == pl.num_programs(1) - 1

        @pl.when(last_f)
        def _finish():
            out_ref[...] = _rms(out_ref[...], gf_ref[...])

        @pl.when(last_f & (i == N_MLP_TILES - 1))
        def _split():
            outs_ref[...] = out_ref[MLP_SAMPLE_ROW0:, :]


def _mlp_layer(layer, h, norm_mlp, w_up, w_down, norm_f=None, convert=()):
    final = norm_f is not None
    n_ff = D_FF // MLP_FF_TILE
    assert N_MLP_TILES * n_ff >= CONVERT_STEPS
    in_specs = [
        pl.BlockSpec((MLP_ROW_TILE, D_MODEL), lambda i, f: (i, 0)),
        pl.BlockSpec((None, 1, D_MODEL), lambda i, f: (layer, 0, 0)),
        pl.BlockSpec((D_MODEL, MLP_FF_TILE), lambda i, f: (0, f)),
        pl.BlockSpec((MLP_FF_TILE, D_MODEL), lambda i, f: (f, 0)),
    ]
    args = [h, norm_mlp, w_up, w_down]
    row_spec = pl.BlockSpec((MLP_ROW_TILE, D_MODEL), lambda i, f: (i, 0))
    if final:
        in_specs.append(pl.BlockSpec((1, D_MODEL), lambda i, f: (0, 0)))
        args.append(norm_f)
        out_specs = [row_spec, pl.BlockSpec((N_SAMPLE, D_MODEL), lambda i, f: (0, 0))]
        out_shape = [jax.ShapeDtypeStruct((N_PROMPT, D_MODEL), _F32),
                     jax.ShapeDtypeStruct((N_SAMPLE, D_MODEL), _F32)]
    else:
        out_specs = [row_spec]
        out_shape = [jax.ShapeDtypeStruct((N_ROWS, D_MODEL), _F32)]
    band = lambda i, f: jnp.minimum(i * n_ff + f, CONVERT_STEPS - 1)
    for stacked, idx in convert:
        _, rows, cols = stacked.shape
        rb = rows // CONVERT_STEPS
        assert rb * CONVERT_STEPS == rows and rb % 16 == 0
        in_specs.append(pl.BlockSpec((None, rb, cols), lambda i, f, idx=idx: (idx, band(i, f), 0)))
        out_specs.append(pl.BlockSpec((rb, cols), lambda i, f: (band(i, f), 0)))
        out_shape.append(jax.ShapeDtypeStruct((rows, cols), _BF16))
        args.append(stacked)
    return pl.pallas_call(
        functools.partial(_mlp_kernel, final=final, n_convert=len(convert)),
        grid=(N_MLP_TILES, n_ff),
        in_specs=in_specs,
        out_specs=out_specs,
        out_shape=out_shape,
        scratch_shapes=[pltpu.VMEM((MLP_ROW_TILE, D_MODEL), _BF16)],
        compiler_params=pltpu.CompilerParams(
            dimension_semantics=("arbitrary", "arbitrary"), vmem_limit_bytes=VMEM_LIMIT),
        name="mlp_layer",
    )(*args)


def _kv_kernel(h_ref, g_ref, w_ref, out_ref):
    hk = _rms(h_ref[...], g_ref[...]).astype(_BF16)
    out_ref[...] = _dot(hk, w_ref[...])


def _kv_proj(h, gain, w_kv):
    return pl.pallas_call(
        _kv_kernel,
        grid=(N_MLP_TILES,),
        in_specs=[
            pl.BlockSpec((MLP_ROW_TILE, D_MODEL), lambda i: (i, 0)),
            pl.BlockSpec((1, D_MODEL), lambda i: (0, 0)),
            pl.BlockSpec((D_MODEL, 2 * KV_DIM), lambda i: (0, 0)),
        ],
        out_specs=pl.BlockSpec((MLP_ROW_TILE, 2 * KV_DIM), lambda i: (i, 0)),
        out_shape=jax.ShapeDtypeStruct((N_ROWS, 2 * KV_DIM), _F32),
        compiler_params=pltpu.CompilerParams(
            dimension_semantics=("arbitrary",), vmem_limit_bytes=VMEM_LIMIT),
        name="kv_proj",
    )(h, gain, w_kv)


def _pair_diag_tall(x_full, g):
    keys = x_full.shape[0]
    c, e0 = divmod(g, 2)
    xa = x_full[:, c * 2 * HEAD_DIM:(c + 1) * 2 * HEAD_DIM]
    xr = pltpu.roll(xa, HEAD_DIM, 1)
    low_half = lax.broadcasted_iota(jnp.int32, (keys, 2 * HEAD_DIM), 1) < HEAD_DIM
    top = jnp.where(low_half, xa if e0 == 0 else xr, 0.0)
    bot = jnp.where(low_half, 0.0, xr if e0 == 0 else xa)
    return jnp.concatenate([top, bot], axis=0).astype(_BF16)


def _pair_diag_wide(xt, g):
    xg = xt[g * HEAD_DIM:(g + 1) * HEAD_DIM, :]
    z = jnp.zeros_like(xg)
    return jnp.concatenate([jnp.concatenate([xg, z], axis=1),
                            jnp.concatenate([z, xg], axis=1)], axis=0)


def _rows_scores(q_groups, k_full):
    kt = k_full.T.astype(_BF16)
    scores = []
    for g in range(N_KV_HEADS):
        qg = q_groups[g]
        qs = jnp.concatenate([qg[:, p * 128:(p + 1) * 128] for p in range(4)], axis=0)
        scores.append(_dot(qs.astype(_BF16), _pair_diag_wide(kt, g)))
    return scores


def _rows_softmax(scores, allowed, tq, sink_ref):
    probs = []
    for g in range(N_KV_HEADS):
        s = scores[g]
        p_rows = []
        for p in range(4):
            p_cols = []
            for e in range(2):
                sink = sink_ref[g * 8 + p * 2 + e]
                spe = s[p * tq:(p + 1) * tq, e * KEY_SLOTS:(e + 1) * KEY_SLOTS]
                spe = jnp.where(allowed, spe, -jnp.inf)
                m = jnp.maximum(jnp.max(spe, axis=-1, keepdims=True), sink)
                pe = jnp.exp(spe - m)
                l = jnp.sum(pe, axis=-1, keepdims=True) + jnp.exp(sink - m)
                p_cols.append(pe * (1.0 / l))
            p_rows.append(jnp.concatenate(p_cols, axis=1))
        probs.append(jnp.concatenate(p_rows, axis=0).astype(_BF16))
    return probs


def _rows_values(probs, v_full, tq):
    outs = []
    for g in range(N_KV_HEADS):
        o = _dot(probs[g], _pair_diag_tall(v_full, g))
        outs.append(jnp.concatenate([o[p * tq:(p + 1) * tq, :] for p in range(4)], axis=1))
    return outs


def _attn_sample_rows(sink_ref, x, kv_ref, ck_ref, cv_ref, g_ref, wq_ref, wo_ref, q_ref, o_ref):
    hn = _rms(x, g_ref[...]).astype(_BF16)
    q = _dot(hn, wq_ref[...]) * (HEAD_DIM ** -0.5)
    for g in range(N_KV_HEADS):
        q_ref[g] = q[:, g * GROUP_LANES:(g + 1) * GROUP_LANES]

    qi = lax.broadcasted_iota(jnp.int32, (DEC_SEQ, KEY_SLOTS), 0)
    kj = lax.broadcasted_iota(jnp.int32, (DEC_SEQ, KEY_SLOTS), 1)
    allowed = ((kj < WINDOW) & (kj > qi)) | ((kj >= WINDOW) & (kj - WINDOW <= qi))
    pad = jnp.zeros((KEY_SLOTS - WINDOW - DEC_SEQ, KV_DIM), _F32)

    def body(j, carry):
        seqs = []
        for u in range(SAMPLE_UNROLL):
            b = j * SAMPLE_UNROLL + u
            rows = pl.ds(pl.multiple_of(b * DEC_SEQ, DEC_SEQ), DEC_SEQ)
            kb = jnp.concatenate([ck_ref[b], kv_ref[rows, 0:KV_DIM], pad], axis=0)
            scores = _rows_scores([q_ref[g, rows, :] for g in range(N_KV_HEADS)], kb)
            seqs.append((b, rows, scores))
        probs = [_rows_softmax(scores, allowed, DEC_SEQ, sink_ref) for _, _, scores in seqs]
        for (b, rows, _), pm in zip(seqs, probs):
            vb = jnp.concatenate([cv_ref[b], kv_ref[rows, KV_DIM:2 * KV_DIM], pad], axis=0)
            outs = _rows_values(pm, vb, DEC_SEQ)
            for g in range(N_KV_HEADS):
                o_ref[g, rows, :] = outs[g]
        return carry

    lax.fori_loop(0, DEC_BATCH // SAMPLE_UNROLL, body, 0)

    acc = x
    for g in range(N_KV_HEADS):
        acc = acc + _dot(o_ref[g].astype(_BF16), wo_ref[g * GROUP_LANES:(g + 1) * GROUP_LANES, :])
    return acc


def _attn_core_lanes(qt_ref, ot_ref, cols, k_full, v_full, allowed_t, sink_ref, between):
    vt = v_full.T.astype(_BF16)

    def scores(g):
        base = g * GROUP_LANES
        qst = jnp.concatenate([qt_ref[base + p * 128:base + (p + 1) * 128, cols]
                               for p in range(4)], axis=1)
        return _dot(_pair_diag_tall(k_full, g), qst)

    def softmax(g, st):
        inv = {}
        pt_rows = []
        for e in range(2):
            pt_cols = []
            for p in range(4):
                sink = sink_ref[g * 8 + p * 2 + e]
                sub = st[e * KEY_SLOTS:(e + 1) * KEY_SLOTS, p * WINDOW:(p + 1) * WINDOW]
                sub = jnp.where(allowed_t, sub, -jnp.inf)
                m = jnp.maximum(jnp.max(sub, axis=0, keepdims=True), sink)
                pe = jnp.exp(sub - m)
                l = jnp.sum(pe, axis=0, keepdims=True) + jnp.exp(sink - m)
                inv[e, p] = 1.0 / l
                pt_cols.append(pe.astype(_BF16))
            pt_rows.append(jnp.concatenate(pt_cols, axis=1))
        return jnp.concatenate(pt_rows, axis=0), inv

    def values(g, pt, inv):
        base = g * GROUP_LANES
        ot = _dot(_pair_diag_wide(vt, g), pt)
        for e in range(2):
            for p in range(4):
                r0 = base + p * 128 + e * HEAD_DIM
                ot_ref[r0:r0 + HEAD_DIM, cols] = (
                    ot[e * HEAD_DIM:(e + 1) * HEAD_DIM, p * WINDOW:(p + 1) * WINDOW] * inv[e, p])

    st = {0: scores(0), 1: scores(1)}
    between()
    for g in range(N_KV_HEADS):
        pt, inv = softmax(g, st.pop(g))
        if g + 2 < N_KV_HEADS:
            st[g + 2] = scores(g + 2)
        values(g, pt, inv)


def _attn_kernel(sink_ref, h_ref, kv_ref, kvp_ref, ck_ref, cv_ref, g_ref, wq_ref, wo_ref,
                 out_ref, qt_ref, ot_ref, q_ref, o_ref):
    i = pl.program_id(0)

    @pl.when(i < N_PROMPT_TILES)
    def _prompt():
        n_blocks = ROW_TILE // WINDOW
        cols = [slice(n * WINDOW, (n + 1) * WINDOW) for n in range(n_blocks)]

        def project_q(n):
            hn = _rms(h_ref[cols[n], :], g_ref[...]).astype(_BF16)
            q = _dot(hn, wq_ref[...]) * (HEAD_DIM ** -0.5)
            qt_ref[:, cols[n]] = q.T.astype(_BF16)

        def project_o(n):
            o = ot_ref[:, cols[n]].T.astype(_BF16)
            out_ref[cols[n], :] = h_ref[cols[n], :] + _dot(o, wo_ref[...])

        def between_blocks(n):
            if n > 0:
                project_o(n - 1)
            if n + 1 < n_blocks:
                project_q(n + 1)

        first = (i % TILES_PER_SEQ) == 0
        kj = lax.broadcasted_iota(jnp.int32, (KEY_SLOTS, WINDOW), 0)
        qi = lax.broadcasted_iota(jnp.int32, (KEY_SLOTS, WINDOW), 1)
        band = (kj > qi) & (kj <= qi + WINDOW)
        project_q(0)
        for n in range(n_blocks):
            allowed_t = band
            if n == 0:
                allowed_t = band & ((kj >= WINDOW) | jnp.logical_not(first))
                kv_prev = kvp_ref[...]
            else:
                kv_prev = kv_ref[(n - 1) * WINDOW:n * WINDOW, :]
            kv_blk = jnp.concatenate([kv_prev, kv_ref[cols[n], :]], axis=0)
            _attn_core_lanes(qt_ref, ot_ref, cols[n], kv_blk[:, 0:KV_DIM], kv_blk[:, KV_DIM:],
                             allowed_t, sink_ref, functools.partial(between_blocks, n))
        project_o(n_blocks - 1)

    @pl.when(i == N_PROMPT_TILES)
    def _sample():
        out_ref[0:N_SAMPLE, :] = _attn_sample_rows(
            sink_ref, h_ref[0:N_SAMPLE, :], kv_ref, ck_ref, cv_ref, g_ref, wq_ref, wo_ref, q_ref, o_ref)
        out_ref[N_SAMPLE:, :] = jnp.zeros((ROW_TILE - N_SAMPLE, D_MODEL), _F32)


def _attn_layer(layer, h, kv, cache_k, cache_v, norm_b, w_q, w_o, sinks):
    kv_blocks_per_tile = ROW_TILE // WINDOW
    const1 = pl.Buffered(1)
    return pl.pallas_call(
        _attn_kernel,
        grid=(N_ROW_TILES,),
        in_specs=[
            pl.BlockSpec(memory_space=pltpu.SMEM),
            pl.BlockSpec((ROW_TILE, D_MODEL), lambda i: (i, 0)),
            pl.BlockSpec((ROW_TILE, 2 * KV_DIM), lambda i: (i, 0)),
            pl.BlockSpec((WINDOW, 2 * KV_DIM),
                         lambda i: (jnp.maximum(i * kv_blocks_per_tile - 1, 0), 0)),
            pl.BlockSpec((DEC_BATCH, WINDOW, KV_DIM), lambda i: (0, 0, 0), pipeline_mode=const1),
            pl.BlockSpec((DEC_BATCH, WINDOW, KV_DIM), lambda i: (0, 0, 0), pipeline_mode=const1),
            pl.BlockSpec((None, 1, D_MODEL), lambda i: (layer, 0, 0)),
            pl.BlockSpec((D_MODEL, D_MODEL), lambda i: (0, 0), pipeline_mode=const1),
            pl.BlockSpec((D_MODEL, D_MODEL), lambda i: (0, 0), pipeline_mode=const1),
        ],
        out_specs=pl.BlockSpec((ROW_TILE, D_MODEL), lambda i: (i, 0)),
        out_shape=jax.ShapeDtypeStruct((N_ROWS, D_MODEL), _F32),
        scratch_shapes=[
            pltpu.VMEM((D_MODEL, ROW_TILE), _BF16),
            pltpu.VMEM((D_MODEL, ROW_TILE), _F32),
            pltpu.VMEM((N_KV_HEADS, N_SAMPLE, GROUP_LANES), _F32),
            pltpu.VMEM((N_KV_HEADS, N_SAMPLE, GROUP_LANES), _F32),
        ],
        compiler_params=pltpu.CompilerParams(
            dimension_semantics=("arbitrary",), vmem_limit_bytes=VMEM_LIMIT),
        name="attn_layer",
    )(sinks, h, kv, kv, cache_k, cache_v, norm_b, w_q, w_o)


def kernel(x_prompt, x_sample, state_pool, cache_k_win, cache_v_win, norm_a, w_pool, pool_scale,
           norm_kv, w_k, w_v, norm_b, w_q, w_o, sinks, norm_mlp, w_up, w_down, norm_f):
    cache_k = cache_k_win.reshape(DEC_BATCH, WINDOW, KV_DIM)
    cache_v = cache_v_win.reshape(DEC_BATCH, WINDOW, KV_DIM)
    rows3 = lambda v: v.reshape(v.shape[0], 1, D_MODEL)
    norm_a3, scale3, norm_b3, norm_mlp3 = rows3(norm_a), rows3(pool_scale), rows3(norm_b), rows3(norm_mlp)
    w_pool_b = w_pool.astype(_BF16)
    w_kv = jnp.concatenate([w_k, w_v], axis=1).astype(_BF16)
    w_up_b, w_down_b = w_up[0].astype(_BF16), w_down[0].astype(_BF16)
    w_q_b = w_o_b = kv = None

    pool_p, pool_s = [], []
    h_prompt = x_prompt.reshape(N_PROMPT, D_MODEL)
    h_sample, sample_block = x_sample.reshape(N_SAMPLE, D_MODEL), 0
    for l in range(DEPTH):
        if l < N_A_LAYERS:
            h, pp, ps = _pool_layer(l, h_prompt, h_sample, sample_block, state_pool, norm_a3,
                                    w_pool_b, scale3)
            pool_p.append(pp)
            pool_s.append(ps)
        else:
            h = _attn_layer(l - N_A_LAYERS, h, kv, cache_k, cache_v, norm_b3, w_q_b, w_o_b,
                            sinks[l - N_A_LAYERS])
        if l == DEPTH - 1:
            y_prompt, y_sample = _mlp_layer(l, h, norm_mlp3, w_up_b, w_down_b,
                                            norm_f=norm_f.reshape(1, D_MODEL))
            break
        convert = [(w_up, l + 1), (w_down, l + 1)]
        if l + 1 >= N_A_LAYERS:
            convert += [(w_q, l + 1 - N_A_LAYERS), (w_o, l + 1 - N_A_LAYERS)]
        h, w_up_b, w_down_b, *w_attn = _mlp_layer(l, h, norm_mlp3, w_up_b, w_down_b, convert=convert)
        if w_attn:
            w_q_b, w_o_b = w_attn
        h_prompt, h_sample, sample_block = h, h, SAMPLE_BLOCK
        if l == N_A_LAYERS - 1:
            kv = _kv_proj(h, norm_kv.reshape(1, D_MODEL), w_kv)

    kv_tail = kv[:N_PROMPT].reshape(BATCH, SEQ, 2 * KV_DIM)[:, SEQ - WINDOW:]
    kv_new = kv[N_PROMPT:]

    def prompt_window(a):
        return a.reshape(BATCH, WINDOW, N_KV_HEADS, HEAD_DIM)

    def sample_window(buf, a):
        new = a.reshape(DEC_BATCH, DEC_SEQ, N_KV_HEADS, HEAD_DIM)
        return jnp.concatenate([buf[:, DEC_SEQ:], new], axis=1)

    return (y_prompt.reshape(BATCH, SEQ, D_MODEL), y_sample.reshape(DEC_BATCH, DEC_SEQ, D_MODEL),
            jnp.stack(pool_p), jnp.stack(pool_s),
            prompt_window(kv_tail[..., :KV_DIM]), prompt_window(kv_tail[..., KV_DIM:]),
            sample_window(cache_k_win, kv_new[:, :KV_DIM]), sample_window(cache_v_win, kv_new[:, KV_DIM:]))
```

```python
import functools

import jax
import jax.numpy as jnp
import numpy as np
from jax import lax
from jax.experimental import pallas as pl
from jax.experimental.pallas import tpu as pltpu

D_MODEL = 2048
BATCH = 4
SEQ = 2048
DEC_BATCH = 32
DEC_SEQ = 8
N_A_LAYERS = 2
N_B_LAYERS = 2
DEPTH = 4
POOL_WINDOWS = (2, 4, 8, 16)
POOL_GROUP = D_MODEL // len(POOL_WINDOWS)
POOL_STATE = 15
HEAD_DIM = 64
N_HEADS = 32
N_KV_HEADS = 4
KV_DIM = N_KV_HEADS * HEAD_DIM
GROUP_LANES = D_MODEL // N_KV_HEADS
WINDOW = 128
D_FF = 4 * D_MODEL
RMS_EPS = 1e-5

N_PROMPT = BATCH * SEQ
N_SAMPLE = DEC_BATCH * DEC_SEQ
N_ROWS = N_PROMPT + N_SAMPLE

ROW_TILE = 512
N_ROW_TILES = -(-N_ROWS // ROW_TILE)
TILES_PER_SEQ = SEQ // ROW_TILE
N_PROMPT_TILES = N_PROMPT // ROW_TILE
SAMPLE_BLOCK = N_PROMPT // N_SAMPLE
HALO = 16
MXU_HALO = 128
MXU_POOL_WINDOWS = (8, 16)
MLP_ROW_TILE = 768
MLP_FF_TILE = 1024
FIRST_FF_TILE = 512
N_MLP_TILES = N_ROWS // MLP_ROW_TILE
MLP_SAMPLE_ROW0 = N_PROMPT - (N_MLP_TILES - 1) * MLP_ROW_TILE
assert N_MLP_TILES * MLP_ROW_TILE == N_ROWS and MLP_SAMPLE_ROW0 + N_SAMPLE == MLP_ROW_TILE
KEY_SLOTS = 2 * WINDOW
SAMPLE_UNROLL = 4
CONVERT_STEPS = 64

VMEM_LIMIT = 56 * 1024 * 1024

_BF16 = jnp.bfloat16
_F32 = jnp.float32


def _rms(x, g):
    ms = jnp.mean(x * x, axis=-1, keepdims=True)
    return (x * lax.rsqrt(ms + RMS_EPS)) * g


def _dot(a, b):
    return jnp.dot(a, b, preferred_element_type=_F32)


def _pool_band_matrices():
    t = np.arange(ROW_TILE)[:, None]
    r = np.arange(MXU_HALO + ROW_TILE)[None, :]
    mats = []
    for w in MXU_POOL_WINDOWS:
        band = ((r > t + MXU_HALO - w) & (r <= t + MXU_HALO)).astype(np.float32)
        mats.append(np.concatenate([band, band], axis=1))
    return jnp.asarray(np.stack(mats), dtype=_BF16)


def _pool_kernel(hp_ref, halo_ref, hs_ref, past_ref, g_ref, w_ref, sc_ref, band_ref,
                 out_ref, poolp_ref, pools_ref, ext_ref, ext3_ref):
    i = pl.program_id(0)
    g = g_ref[...]

    @pl.when(i == 0)
    def _zero_unused_lookback():
        ext_ref[0:MXU_HALO - HALO, :] = jnp.zeros((MXU_HALO - HALO, D_MODEL), _F32)

    @pl.when(i < N_PROMPT_TILES)
    def _prompt():
        x = hp_ref[...]
        u = _rms(x, g)
        first = (i % TILES_PER_SEQ) == 0
        halo = jnp.where(first, 0.0, _rms(halo_ref[...], g))
        ext_ref[MXU_HALO - HALO:MXU_HALO, :] = halo
        ext_ref[MXU_HALO:MXU_HALO + ROW_TILE, :] = u
        poolp_ref[0] = ext_ref[MXU_HALO + ROW_TILE - POOL_STATE:MXU_HALO + ROW_TILE, :]
        pos = (i % TILES_PER_SEQ) * ROW_TILE + lax.broadcasted_iota(jnp.int32, (ROW_TILE, 1), 0)
        groups = [slice(gi * POOL_GROUP, (gi + 1) * POOL_GROUP) for gi in range(len(POOL_WINDOWS))]
        mxu_sums = {}
        for k, w in enumerate(MXU_POOL_WINDOWS):
            e = ext_ref[:, groups[POOL_WINDOWS.index(w)]]
            hi = e.astype(_BF16)
            lo = (e - hi.astype(_F32)).astype(_BF16)
            mxu_sums[w] = _dot(band_ref[k], jnp.concatenate([hi, lo], axis=0))
        for gi, w in enumerate(POOL_WINDOWS):
            sl = groups[gi]
            ug = u[:, sl]
            if w in mxu_sums:
                s = mxu_sums[w]
            else:
                s = ug
                for j in range(1, w):
                    s = s + ext_ref[MXU_HALO - j:MXU_HALO - j + ROW_TILE, sl]
            inv = 1.0 / jnp.minimum(pos + 1, w).astype(_F32)
            d = s * inv - ug
            o = _dot(d.astype(_BF16), w_ref[gi])
            out_ref[:, sl] = x[:, sl] + o * sc_ref[:, sl]

    @pl.when(i == N_PROMPT_TILES)
    def _sample():
        x = hs_ref[...]
        u = _rms(x, g)
        u3 = u.reshape(DEC_BATCH, DEC_SEQ, D_MODEL)
        ext3_ref[:, HALO - POOL_STATE:HALO, :] = past_ref[...]
        ext3_ref[:, HALO:HALO + DEC_SEQ, :] = u3
        pools_ref[:, 0:POOL_STATE - DEC_SEQ, :] = past_ref[:, DEC_SEQ:POOL_STATE, :]
        pools_ref[:, POOL_STATE - DEC_SEQ:POOL_STATE, :] = u3
        for gi, w in enumerate(POOL_WINDOWS):
            sl = slice(gi * POOL_GROUP, (gi + 1) * POOL_GROUP)
            ug = u3[:, :, sl]
            s = ug
            for j in range(1, w):
                s = s + ext3_ref[:, HALO - j:HALO - j + DEC_SEQ, sl]
            d = s * (1.0 / w) - ug
            o = _dot(d.reshape(N_SAMPLE, POOL_GROUP).astype(_BF16), w_ref[gi])
            out_ref[0:N_SAMPLE, sl] = x[:, sl] + o * sc_ref[:, sl]
        out_ref[N_SAMPLE:, :] = jnp.zeros((ROW_TILE - N_SAMPLE, D_MODEL), _F32)


def _pool_layer(layer, h_prompt, h_sample, sample_block, state_pool, norm_a, w_pool, pool_scale):
    halo_blocks_per_tile = ROW_TILE // HALO
    return pl.pallas_call(
        _pool_kernel,
        grid=(N_ROW_TILES,),
        in_specs=[
            pl.BlockSpec((ROW_TILE, D_MODEL), lambda i: (jnp.minimum(i, N_PROMPT_TILES - 1), 0)),
            pl.BlockSpec((HALO, D_MODEL),
                         lambda i: (jnp.clip(i * halo_blocks_per_tile - 1, 0,
                                             N_PROMPT // HALO - 1), 0)),
            pl.BlockSpec((N_SAMPLE, D_MODEL), lambda i: (sample_block, 0)),
            pl.BlockSpec((None, DEC_BATCH, POOL_STATE, D_MODEL), lambda i: (layer, 0, 0, 0),
                         pipeline_mode=pl.Buffered(1)),
            pl.BlockSpec((None, 1, D_MODEL), lambda i: (layer, 0, 0)),
            pl.BlockSpec((None, len(POOL_WINDOWS), POOL_GROUP, POOL_GROUP),
                         lambda i: (layer, 0, 0, 0)),
            pl.BlockSpec((None, 1, D_MODEL), lambda i: (layer, 0, 0)),
            pl.BlockSpec((len(MXU_POOL_WINDOWS), ROW_TILE, 2 * (MXU_HALO + ROW_TILE)),
                         lambda i: (0, 0, 0)),
        ],
        out_specs=[
            pl.BlockSpec((ROW_TILE, D_MODEL), lambda i: (i, 0)),
            pl.BlockSpec((1, POOL_STATE, D_MODEL),
                         lambda i: (jnp.minimum(i // TILES_PER_SEQ, BATCH - 1), 0, 0)),
            pl.BlockSpec((DEC_BATCH, POOL_STATE, D_MODEL), lambda i: (0, 0, 0)),
        ],
        out_shape=[
            jax.ShapeDtypeStruct((N_ROWS, D_MODEL), _F32),
            jax.ShapeDtypeStruct((BATCH, POOL_STATE, D_MODEL), _F32),
            jax.ShapeDtypeStruct((DEC_BATCH, POOL_STATE, D_MODEL), _F32),
        ],
        scratch_shapes=[
            pltpu.VMEM((MXU_HALO + ROW_TILE, D_MODEL), _F32),
            pltpu.VMEM((DEC_BATCH, HALO + DEC_SEQ, D_MODEL), _F32),
        ],
        compiler_params=pltpu.CompilerParams(
            dimension_semantics=("arbitrary",), vmem_limit_bytes=VMEM_LIMIT),
        name="pool_layer",
    )(h_prompt, h_prompt, h_sample, state_pool, norm_a, w_pool, pool_scale, _pool_band_matrices())


def _mlp_first_kernel(h_ref, g_ref, wup_ref, wdn_ref, *rest, final):
    if final:
        gf_ref, out_ref, wub_ref, wdb_ref, hn_ref = rest
    else:
        out_ref, wub_ref, wdb_ref, hn_ref = rest
    f = pl.program_id(0)

    @pl.when(f == 0)
    def _init():
        x = h_ref[...]
        hn_ref[...] = _rms(x, g_ref[...]).astype(_BF16)
        out_ref[...] = x

    wu = wup_ref[...].astype(_BF16)
    wd = wdn_ref[...].astype(_BF16)
    wub_ref[...] = wu
    wdb_ref[...] = wd
    a = jnp.maximum(_dot(hn_ref[...], wu), 0.0)
    out_ref[...] += _dot((a * a).astype(_BF16), wd)

    if final:
        @pl.when(f == pl.num_programs(0) - 1)
        def _finish():
            out_ref[...] = _rms(out_ref[...], gf_ref[...])


def _mlp_first_tile(layer, h, norm_mlp, w_up, w_down, norm_f=None):
    final = norm_f is not None
    in_specs = [
        pl.BlockSpec((MLP_ROW_TILE, D_MODEL), lambda f: (0, 0), pipeline_mode=pl.Buffered(1)),
        pl.BlockSpec((None, 1, D_MODEL), lambda f: (layer, 0, 0)),
        pl.BlockSpec((None, D_MODEL, FIRST_FF_TILE), lambda f: (layer, 0, f)),
        pl.BlockSpec((None, FIRST_FF_TILE, D_MODEL), lambda f: (layer, f, 0)),
    ]
    args = [h, norm_mlp, w_up, w_down]
    if final:
        in_specs.append(pl.BlockSpec((1, D_MODEL), lambda f: (0, 0)))
        args.append(norm_f)
    return pl.pallas_call(
        functools.partial(_mlp_first_kernel, final=final),
        grid=(D_FF // FIRST_FF_TILE,),
        in_specs=in_specs,
        out_specs=[
            pl.BlockSpec((MLP_ROW_TILE, D_MODEL), lambda f: (0, 0)),
            pl.BlockSpec((D_MODEL, FIRST_FF_TILE), lambda f: (0, f)),
            pl.BlockSpec((FIRST_FF_TILE, D_MODEL), lambda f: (f, 0)),
        ],
        out_shape=[
            jax.ShapeDtypeStruct((N_PROMPT if final else N_ROWS, D_MODEL), _F32),
            jax.ShapeDtypeStruct((D_MODEL, D_FF), _BF16),
            jax.ShapeDtypeStruct((D_FF, D_MODEL), _BF16),
        ],
        scratch_shapes=[pltpu.VMEM((MLP_ROW_TILE, D_MODEL), _BF16)],
        compiler_params=pltpu.CompilerParams(
            dimension_semantics=("arbitrary",), vmem_limit_bytes=VMEM_LIMIT),
        name="mlp_first_tile",
    )(*args)


def _mlp_rest_kernel(*refs, final, n_convert):
    h_ref, g_ref, wup_ref, wdn_ref = refs[1:5]
    n_in = 5 + int(final) + n_convert
    gf_ref = refs[5] if final else None
    convert_src = refs[n_in - n_convert:n_in]
    out_ref = refs[n_in]
    outs_ref = refs[n_in + 1] if final else None
    n_out = 1 + int(final)
    convert_dst = refs[n_in + n_out:n_in + n_out + n_convert]
    hn_ref = refs[n_in + n_out + n_convert]
    i = pl.program_id(0)
    f = pl.program_id(1)

    @pl.when(f == 0)
    def _init():
        x = h_ref[...]
        hn_ref[...] = _rms(x, g_ref[...]).astype(_BF16)
        out_ref[...] = x

    a = jnp.maximum(_dot(hn_ref[...], wup_ref[...]), 0.0)
    out_ref[...] += _dot((a * a).astype(_BF16), wdn_ref[...])

    if n_convert:
        @pl.when(i * pl.num_programs(1) + f < CONVERT_STEPS)
        def _convert():
            for src, dst in zip(convert_src, convert_dst):
                dst[...] = src[...].astype(_BF16)

    if final:
        last_f = f == pl.num_programs(1) - 1

        @pl.when(last_f)
        def _finish():
            out_ref[...] = _rms(out_ref[...], gf_ref[...])

        @pl.when(last_f & (i == pl.num_programs(0) - 1))
        def _split():
            outs_ref[...] = out_ref[MLP_SAMPLE_ROW0:, :]


def _mlp_rest_tiles(layer, h, first, norm_mlp, w_up, w_down, norm_f=None, convert=()):
    final = norm_f is not None
    n_ff = D_FF // MLP_FF_TILE
    assert (N_MLP_TILES - 1) * n_ff >= CONVERT_STEPS
    in_specs = [
        pl.BlockSpec(memory_space=pl.ANY),
        pl.BlockSpec((MLP_ROW_TILE, D_MODEL), lambda i, f: (i + 1, 0)),
        pl.BlockSpec((None, 1, D_MODEL), lambda i, f: (layer, 0, 0)),
        pl.BlockSpec((D_MODEL, MLP_FF_TILE), lambda i, f: (0, f)),
        pl.BlockSpec((MLP_FF_TILE, D_MODEL), lambda i, f: (f, 0)),
    ]
    args = [first, h, norm_mlp, w_up, w_down]
    row_spec = pl.BlockSpec((MLP_ROW_TILE, D_MODEL), lambda i, f: (i + 1, 0))
    if final:
        in_specs.append(pl.BlockSpec((1, D_MODEL), lambda i, f: (0, 0)))
        args.append(norm_f)
        out_specs = [row_spec, pl.BlockSpec((N_SAMPLE, D_MODEL), lambda i, f: (0, 0))]
        out_shape = [jax.ShapeDtypeStruct((N_PROMPT, D_MODEL), _F32),
                     jax.ShapeDtypeStruct((N_SAMPLE, D_MODEL), _F32)]
    else:
        out_specs = [row_spec]
        out_shape = [jax.ShapeDtypeStruct((N_ROWS, D_MODEL), _F32)]
    band = lambda i, f: jnp.minimum(i * n_ff + f, CONVERT_STEPS - 1)
    for stacked, idx in convert:
        _, rows, cols = stacked.shape
        rb = rows // CONVERT_STEPS
        assert rb * CONVERT_STEPS == rows and rb % 16 == 0
        in_specs.append(pl.BlockSpec((None, rb, cols), lambda i, f, idx=idx: (idx, band(i, f), 0)))
        out_specs.append(pl.BlockSpec((rb, cols), lambda i, f: (band(i, f), 0)))
        out_shape.append(jax.ShapeDtypeStruct((rows, cols), _BF16))
        args.append(stacked)
    return pl.pallas_call(
        functools.partial(_mlp_rest_kernel, final=final, n_convert=len(convert)),
        grid=(N_MLP_TILES - 1, n_ff),
        in_specs=in_specs,
        out_specs=out_specs,
        out_shape=out_shape,
        input_output_aliases={0: 0},
        scratch_shapes=[pltpu.VMEM((MLP_ROW_TILE, D_MODEL), _BF16)],
        compiler_params=pltpu.CompilerParams(
            dimension_semantics=("arbitrary", "arbitrary"), vmem_limit_bytes=VMEM_LIMIT),
        name="mlp_rest_tiles",
    )(*args)


def _mlp_layer(layer, h, norm_mlp, w_up, w_down, norm_f=None, convert=()):
    first, w_up_b, w_down_b = _mlp_first_tile(layer, h, norm_mlp, w_up, w_down, norm_f)
    return _mlp_rest_tiles(layer, h, first, norm_mlp, w_up_b, w_down_b, norm_f, convert)


def _kv_kernel(h_ref, g_ref, w_ref, out_ref):
    hk = _rms(h_ref[...], g_ref[...]).astype(_BF16)
    out_ref[...] = _dot(hk, w_ref[...])


def _kv_proj(h, gain, w_kv):
    return pl.pallas_call(
        _kv_kernel,
        grid=(N_MLP_TILES,),
        in_specs=[
            pl.BlockSpec((MLP_ROW_TILE, D_MODEL), lambda i: (i, 0)),
            pl.BlockSpec((1, D_MODEL), lambda i: (0, 0)),
            pl.BlockSpec((D_MODEL, 2 * KV_DIM), lambda i: (0, 0)),
        ],
        out_specs=pl.BlockSpec((MLP_ROW_TILE, 2 * KV_DIM), lambda i: (i, 0)),
        out_shape=jax.ShapeDtypeStruct((N_ROWS, 2 * KV_DIM), _F32),
        compiler_params=pltpu.CompilerParams(
            dimension_semantics=("arbitrary",), vmem_limit_bytes=VMEM_LIMIT),
        name="kv_proj",
    )(h, gain, w_kv)


def _pair_diag_tall(x_full, g):
    keys = x_full.shape[0]
    c, e0 = divmod(g, 2)
    xa = x_full[:, c * 2 * HEAD_DIM:(c + 1) * 2 * HEAD_DIM]
    xr = pltpu.roll(xa, HEAD_DIM, 1)
    low_half = lax.broadcasted_iota(jnp.int32, (keys, 2 * HEAD_DIM), 1) < HEAD_DIM
    top = jnp.where(low_half, xa if e0 == 0 else xr, 0.0)
    bot = jnp.where(low_half, 0.0, xr if e0 == 0 else xa)
    return jnp.concatenate([top, bot], axis=0).astype(_BF16)


def _pair_diag_wide(xt, g):
    xg = xt[g * HEAD_DIM:(g + 1) * HEAD_DIM, :]
    z = jnp.zeros_like(xg)
    return jnp.concatenate([jnp.concatenate([xg, z], axis=1),
                            jnp.concatenate([z, xg], axis=1)], axis=0)


def _rows_scores(q_groups, k_full):
    kt = k_full.T.astype(_BF16)
    scores = []
    for g in range(N_KV_HEADS):
        qg = q_groups[g]
        qs = jnp.concatenate([qg[:, p * 128:(p + 1) * 128] for p in range(4)], axis=0)
        scores.append(_dot(qs.astype(_BF16), _pair_diag_wide(kt, g)))
    return scores


def _rows_softmax(scores, allowed, tq, sink_ref):
    probs = []
    for g in range(N_KV_HEADS):
        s = scores[g]
        p_rows = []
        for p in range(4):
            p_cols = []
            for e in range(2):
                sink = sink_ref[g * 8 + p * 2 + e]
                spe = s[p * tq:(p + 1) * tq, e * KEY_SLOTS:(e + 1) * KEY_SLOTS]
                spe = jnp.where(allowed, spe, -jnp.inf)
                m = jnp.maximum(jnp.max(spe, axis=-1, keepdims=True), sink)
                pe = jnp.exp(spe - m)
                l = jnp.sum(pe, axis=-1, keepdims=True) + jnp.exp(sink - m)
                p_cols.append(pe * (1.0 / l))
            p_rows.append(jnp.concatenate(p_cols, axis=1))
        probs.append(jnp.concatenate(p_rows, axis=0).astype(_BF16))
    return probs


def _rows_values(probs, v_full, tq):
    outs = []
    for g in range(N_KV_HEADS):
        o = _dot(probs[g], _pair_diag_tall(v_full, g))
        outs.append(jnp.concatenate([o[p * tq:(p + 1) * tq, :] for p in range(4)], axis=1))
    return outs


def _attn_sample_rows(sink_ref, x, kv_ref, ck_ref, cv_ref, g_ref, wq_ref, wo_ref, q_ref, o_ref):
    hn = _rms(x, g_ref[...]).astype(_BF16)
    q = _dot(hn, wq_ref[...]) * (HEAD_DIM ** -0.5)
    for g in range(N_KV_HEADS):
        q_ref[g] = q[:, g * GROUP_LANES:(g + 1) * GROUP_LANES]

    qi = lax.broadcasted_iota(jnp.int32, (DEC_SEQ, KEY_SLOTS), 0)
    kj = lax.broadcasted_iota(jnp.int32, (DEC_SEQ, KEY_SLOTS), 1)
    allowed = ((kj < WINDOW) & (kj > qi)) | ((kj >= WINDOW) & (kj - WINDOW <= qi))
    pad = jnp.zeros((KEY_SLOTS - WINDOW - DEC_SEQ, KV_DIM), _F32)

    def body(j, carry):
        seqs = []
        for u in range(SAMPLE_UNROLL):
            b = j * SAMPLE_UNROLL + u
            rows = pl.ds(pl.multiple_of(b * DEC_SEQ, DEC_SEQ), DEC_SEQ)
            kb = jnp.concatenate([ck_ref[b], kv_ref[rows, 0:KV_DIM], pad], axis=0)
            scores = _rows_scores([q_ref[g, rows, :] for g in range(N_KV_HEADS)], kb)
            seqs.append((b, rows, scores))
        probs = [_rows_softmax(scores, allowed, DEC_SEQ, sink_ref) for _, _, scores in seqs]
        for (b, rows, _), pm in zip(seqs, probs):
            vb = jnp.concatenate([cv_ref[b], kv_ref[rows, KV_DIM:2 * KV_DIM], pad], axis=0)
            outs = _rows_values(pm, vb, DEC_SEQ)
            for g in range(N_KV_HEADS):
                o_ref[g, rows, :] = outs[g]
        return carry

    lax.fori_loop(0, DEC_BATCH // SAMPLE_UNROLL, body, 0)

    acc = x
    for g in range(N_KV_HEADS):
        acc = acc + _dot(o_ref[g].astype(_BF16), wo_ref[g * GROUP_LANES:(g + 1) * GROUP_LANES, :])
    return acc


def _attn_core_lanes(qt_ref, ot_ref, cols, k_full, v_full, allowed_t, sink_ref, between):
    vt = v_full.T.astype(_BF16)

    def scores(g):
        base = g * GROUP_LANES
        qst = jnp.concatenate([qt_ref[base + p * 128:base + (p + 1) * 128, cols]
                               for p in range(4)], axis=1)
        return _dot(_pair_diag_tall(k_full, g), qst)

    def softmax(g, st):
        inv = {}
        pt_rows = []
        for e in range(2):
            pt_cols = []
            for p in range(4):
                sink = sink_ref[g * 8 + p * 2 + e]
                sub = st[e * KEY_SLOTS:(e + 1) * KEY_SLOTS, p * WINDOW:(p + 1) * WINDOW]
                sub = jnp.where(allowed_t, sub, -jnp.inf)
                m = jnp.maximum(jnp.max(sub, axis=0, keepdims=True), sink)
                pe = jnp.exp(sub - m)
                l = jnp.sum(pe, axis=0, keepdims=True) + jnp.exp(sink - m)
                inv[e, p] = 1.0 / l
                pt_cols.append(pe.astype(_BF16))
            pt_rows.append(jnp.concatenate(pt_cols, axis=1))
        return jnp.concatenate(pt_rows, axis=0), inv

    def values(g, pt, inv):
        base = g * GROUP_LANES
        ot = _dot(_pair_diag_wide(vt, g), pt)
        for e in range(2):
            for p in range(4):
                r0 = base + p * 128 + e * HEAD_DIM
                ot_ref[r0:r0 + HEAD_DIM, cols] = (
                    ot[e * HEAD_DIM:(e + 1) * HEAD_DIM, p * WINDOW:(p + 1) * WINDOW] * inv[e, p])

    st = {0: scores(0), 1: scores(1)}
    between()
    for g in range(N_KV_HEADS):
        pt, inv = softmax(g, st.pop(g))
        if g + 2 < N_KV_HEADS:
            st[g + 2] = scores(g + 2)
        values(g, pt, inv)


def _attn_kernel(sink_ref, h_ref, kv_ref, kvp_ref, ck_ref, cv_ref, g_ref, wq_ref, wo_ref,
                 out_ref, qt_ref, ot_ref, q_ref, o_ref):
    i = pl.program_id(0)

    @pl.when(i < N_PROMPT_TILES)
    def _prompt():
        n_blocks = ROW_TILE // WINDOW
        cols = [slice(n * WINDOW, (n + 1) * WINDOW) for n in range(n_blocks)]

        def project_q(n):
            hn = _rms(h_ref[cols[n], :], g_ref[...]).astype(_BF16)
            q = _dot(hn, wq_ref[...]) * (HEAD_DIM ** -0.5)
            qt_ref[:, cols[n]] = q.T.astype(_BF16)

        def project_o(n):
            o = ot_ref[:, cols[n]].T.astype(_BF16)
            out_ref[cols[n], :] = h_ref[cols[n], :] + _dot(o, wo_ref[...])

        def between_blocks(n):
            if n > 0:
                project_o(n - 1)
            if n + 1 < n_blocks:
                project_q(n + 1)

        first = (i % TILES_PER_SEQ) == 0
        kj = lax.broadcasted_iota(jnp.int32, (KEY_SLOTS, WINDOW), 0)
        qi = lax.broadcasted_iota(jnp.int32, (KEY_SLOTS, WINDOW), 1)
        band = (kj > qi) & (kj <= qi + WINDOW)
        project_q(0)
        for n in range(n_blocks):
            allowed_t = band
            if n == 0:
                allowed_t = band & ((kj >= WINDOW) | jnp.logical_not(first))
                kv_prev = kvp_ref[...]
            else:
                kv_prev = kv_ref[(n - 1) * WINDOW:n * WINDOW, :]
            kv_blk = jnp.concatenate([kv_prev, kv_ref[cols[n], :]], axis=0)
            _attn_core_lanes(qt_ref, ot_ref, cols[n], kv_blk[:, 0:KV_DIM], kv_blk[:, KV_DIM:],
                             allowed_t, sink_ref, functools.partial(between_blocks, n))
        project_o(n_blocks - 1)

    @pl.when(i == N_PROMPT_TILES)
    def _sample():
        out_ref[0:N_SAMPLE, :] = _attn_sample_rows(
            sink_ref, h_ref[0:N_SAMPLE, :], kv_ref, ck_ref, cv_ref, g_ref, wq_ref, wo_ref, q_ref, o_ref)
        out_ref[N_SAMPLE:, :] = jnp.zeros((ROW_TILE - N_SAMPLE, D_MODEL), _F32)


def _attn_layer(layer, h, kv, cache_k, cache_v, norm_b, w_q, w_o, sinks):
    kv_blocks_per_tile = ROW_TILE // WINDOW
    const1 = pl.Buffered(1)
    return pl.pallas_call(
        _attn_kernel,
        grid=(N_ROW_TILES,),
        in_specs=[
            pl.BlockSpec(memory_space=pltpu.SMEM),
            pl.BlockSpec((ROW_TILE, D_MODEL), lambda i: (i, 0)),
            pl.BlockSpec((ROW_TILE, 2 * KV_DIM), lambda i: (i, 0)),
            pl.BlockSpec((WINDOW, 2 * KV_DIM),
                         lambda i: (jnp.maximum(i * kv_blocks_per_tile - 1, 0), 0)),
            pl.BlockSpec((DEC_BATCH, WINDOW, KV_DIM), lambda i: (0, 0, 0), pipeline_mode=const1),
            pl.BlockSpec((DEC_BATCH, WINDOW, KV_DIM), lambda i: (0, 0, 0), pipeline_mode=const1),
            pl.BlockSpec((None, 1, D_MODEL), lambda i: (layer, 0, 0)),
            pl.BlockSpec((D_MODEL, D_MODEL), lambda i: (0, 0), pipeline_mode=const1),
            pl.BlockSpec((D_MODEL, D_MODEL), lambda i: (0, 0), pipeline_mode=const1),
        ],
        out_specs=pl.BlockSpec((ROW_TILE, D_MODEL), lambda i: (i, 0)),
        out_shape=jax.ShapeDtypeStruct((N_ROWS, D_MODEL), _F32),
        scratch_shapes=[
            pltpu.VMEM((D_MODEL, ROW_TILE), _BF16),
            pltpu.VMEM((D_MODEL, ROW_TILE), _F32),
            pltpu.VMEM((N_KV_HEADS, N_SAMPLE, GROUP_LANES), _F32),
            pltpu.VMEM((N_KV_HEADS, N_SAMPLE, GROUP_LANES), _F32),
        ],
        compiler_params=pltpu.CompilerParams(
            dimension_semantics=("arbitrary",), vmem_limit_bytes=VMEM_LIMIT),
        name="attn_layer",
    )(sinks, h, kv, kv, cache_k, cache_v, norm_b, w_q, w_o)


def kernel(x_prompt, x_sample, state_pool, cache_k_win, cache_v_win, norm_a, w_pool, pool_scale,
           norm_kv, w_k, w_v, norm_b, w_q, w_o, sinks, norm_mlp, w_up, w_down, norm_f):
    cache_k = cache_k_win.reshape(DEC_BATCH, WINDOW, KV_DIM)
    cache_v = cache_v_win.reshape(DEC_BATCH, WINDOW, KV_DIM)
    rows3 = lambda v: v.reshape(v.shape[0], 1, D_MODEL)
    norm_a3, scale3, norm_b3, norm_mlp3 = rows3(norm_a), rows3(pool_scale), rows3(norm_b), rows3(norm_mlp)
    w_pool_b = w_pool.astype(_BF16)
    w_kv = jnp.concatenate([w_k, w_v], axis=1).astype(_BF16)
    w_q_b = w_o_b = kv = None

    pool_p, pool_s = [], []
    h_prompt = x_prompt.reshape(N_PROMPT, D_MODEL)
    h_sample, sample_block = x_sample.reshape(N_SAMPLE, D_MODEL), 0
    for l in range(DEPTH):
        if l < N_A_LAYERS:
            h, pp, ps = _pool_layer(l, h_prompt, h_sample, sample_block, state_pool, norm_a3,
                                    w_pool_b, scale3)
            pool_p.append(pp)
            pool_s.append(ps)
        else:
            h = _attn_layer(l - N_A_LAYERS, h, kv, cache_k, cache_v, norm_b3, w_q_b, w_o_b,
                            sinks[l - N_A_LAYERS])
        if l == DEPTH - 1:
            y_prompt, y_sample = _mlp_layer(l, h, norm_mlp3, w_up, w_down,
                                            norm_f=norm_f.reshape(1, D_MODEL))
            break
        j = l + 1 - N_A_LAYERS
        convert = [(w_q, j), (w_o, j)] if j >= 0 else []
        h, *w_attn = _mlp_layer(l, h, norm_mlp3, w_up, w_down, convert=convert)
        if w_attn:
            w_q_b, w_o_b = w_attn
        h_prompt, h_sample, sample_block = h, h, SAMPLE_BLOCK
        if l == N_A_LAYERS - 1:
            kv = _kv_proj(h, norm_kv.reshape(1, D_MODEL), w_kv)

    kv_tail = kv[:N_PROMPT].reshape(BATCH, SEQ, 2 * KV_DIM)[:, SEQ - WINDOW:]
    kv_new = kv[N_PROMPT:]

    def prompt_window(a):
        return a.reshape(BATCH, WINDOW, N_KV_HEADS, HEAD_DIM)

    def sample_window(buf, a):
        new = a.reshape(DEC_BATCH, DEC_SEQ, N_KV_HEADS, HEAD_DIM)
        return jnp.concatenate([buf[:, DEC_SEQ:], new], axis=1)

    return (y_prompt.reshape(BATCH, SEQ, D_MODEL), y_sample.reshape(DEC_BATCH, DEC_SEQ, D_MODEL),
            jnp.stack(pool_p), jnp.stack(pool_s),
            prompt_window(kv_tail[..., :KV_DIM]), prompt_window(kv_tail[..., KV_DIM:]),
            sample_window(cache_k_win, kv_new[:, :KV_DIM]), sample_window(cache_v_win, kv_new[:, KV_DIM:]))
```

```python
import functools

import jax
import jax.numpy as jnp
import numpy as np
from jax import lax
from jax.experimental import pallas as pl
from jax.experimental.pallas import tpu as pltpu

D_MODEL = 2048
BATCH = 4
SEQ = 2048
DEC_BATCH = 32
DEC_SEQ = 8
N_A_LAYERS = 2
N_B_LAYERS = 2
DEPTH = 4
POOL_WINDOWS = (2, 4, 8, 16)
POOL_GROUP = D_MODEL // len(POOL_WINDOWS)
POOL_STATE = 15
HEAD_DIM = 64
N_HEADS = 32
N_KV_HEADS = 4
KV_DIM = N_KV_HEADS * HEAD_DIM
GROUP_LANES = D_MODEL // N_KV_HEADS
WINDOW = 128
D_FF = 4 * D_MODEL
RMS_EPS = 1e-5

N_PROMPT = BATCH * SEQ
N_SAMPLE = DEC_BATCH * DEC_SEQ
N_ROWS = N_PROMPT + N_SAMPLE

ROW_TILE = 512
N_ROW_TILES = -(-N_ROWS // ROW_TILE)
TILES_PER_SEQ = SEQ // ROW_TILE
N_PROMPT_TILES = N_PROMPT // ROW_TILE
SAMPLE_BLOCK = N_PROMPT // N_SAMPLE
HALO = 16
MXU_HALO = 128
MXU_POOL_WINDOWS = (8, 16)
MLP_ROW_TILE = 768
MLP_FF_TILE = 1024
FIRST_FF_TILE = 512
FF_SUBBLOCKS = MLP_FF_TILE // FIRST_FF_TILE
N_MLP_TILES = N_ROWS // MLP_ROW_TILE
MLP_SAMPLE_ROW0 = N_PROMPT - (N_MLP_TILES - 1) * MLP_ROW_TILE
assert N_MLP_TILES * MLP_ROW_TILE == N_ROWS and MLP_SAMPLE_ROW0 + N_SAMPLE == MLP_ROW_TILE
KEY_SLOTS = 2 * WINDOW
SAMPLE_UNROLL = 4
CONVERT_STEPS = 64

VMEM_LIMIT = 56 * 1024 * 1024

_BF16 = jnp.bfloat16
_F32 = jnp.float32


def _rms(x, g):
    ms = jnp.mean(x * x, axis=-1, keepdims=True)
    return (x * lax.rsqrt(ms + RMS_EPS)) * g


def _dot(a, b):
    return jnp.dot(a, b, preferred_element_type=_F32)


def _pool_band_matrices():
    t = np.arange(ROW_TILE)[:, None]
    r = np.arange(MXU_HALO + ROW_TILE)[None, :]
    mats = []
    for w in MXU_POOL_WINDOWS:
        band = ((r > t + MXU_HALO - w) & (r <= t + MXU_HALO)).astype(np.float32)
        mats.append(np.concatenate([band, band], axis=1))
    return jnp.asarray(np.stack(mats), dtype=_BF16)


def _pool_kernel(hp_ref, halo_ref, hs_ref, past_ref, g_ref, w_ref, sc_ref, band_ref,
                 out_ref, poolp_ref, pools_ref, ext_ref, ext3_ref):
    i = pl.program_id(0)
    g = g_ref[...]

    @pl.when(i == 0)
    def _zero_unused_lookback():
        ext_ref[0:MXU_HALO - HALO, :] = jnp.zeros((MXU_HALO - HALO, D_MODEL), _F32)

    @pl.when(i < N_PROMPT_TILES)
    def _prompt():
        x = hp_ref[...]
        u = _rms(x, g)
        first = (i % TILES_PER_SEQ) == 0
        halo = jnp.where(first, 0.0, _rms(halo_ref[...], g))
        ext_ref[MXU_HALO - HALO:MXU_HALO, :] = halo
        ext_ref[MXU_HALO:MXU_HALO + ROW_TILE, :] = u
        poolp_ref[0] = ext_ref[MXU_HALO + ROW_TILE - POOL_STATE:MXU_HALO + ROW_TILE, :]
        pos = (i % TILES_PER_SEQ) * ROW_TILE + lax.broadcasted_iota(jnp.int32, (ROW_TILE, 1), 0)
        groups = [slice(gi * POOL_GROUP, (gi + 1) * POOL_GROUP) for gi in range(len(POOL_WINDOWS))]
        mxu_sums = {}
        for k, w in enumerate(MXU_POOL_WINDOWS):
            e = ext_ref[:, groups[POOL_WINDOWS.index(w)]]
            hi = e.astype(_BF16)
            lo = (e - hi.astype(_F32)).astype(_BF16)
            mxu_sums[w] = _dot(band_ref[k], jnp.concatenate([hi, lo], axis=0))
        for gi, w in enumerate(POOL_WINDOWS):
            sl = groups[gi]
            ug = u[:, sl]
            if w in mxu_sums:
                s = mxu_sums[w]
            else:
                s = ug
                for j in range(1, w):
                    s = s + ext_ref[MXU_HALO - j:MXU_HALO - j + ROW_TILE, sl]
            inv = 1.0 / jnp.minimum(pos + 1, w).astype(_F32)
            d = s * inv - ug
            o = _dot(d.astype(_BF16), w_ref[gi])
            out_ref[:, sl] = x[:, sl] + o * sc_ref[:, sl]

    @pl.when(i == N_PROMPT_TILES)
    def _sample():
        x = hs_ref[...]
        u = _rms(x, g)
        u3 = u.reshape(DEC_BATCH, DEC_SEQ, D_MODEL)
        ext3_ref[:, HALO - POOL_STATE:HALO, :] = past_ref[...]
        ext3_ref[:, HALO:HALO + DEC_SEQ, :] = u3
        pools_ref[:, 0:POOL_STATE - DEC_SEQ, :] = past_ref[:, DEC_SEQ:POOL_STATE, :]
        pools_ref[:, POOL_STATE - DEC_SEQ:POOL_STATE, :] = u3
        for gi, w in enumerate(POOL_WINDOWS):
            sl = slice(gi * POOL_GROUP, (gi + 1) * POOL_GROUP)
            ug = u3[:, :, sl]
            s = ug
            for j in range(1, w):
                s = s + ext3_ref[:, HALO - j:HALO - j + DEC_SEQ, sl]
            d = s * (1.0 / w) - ug
            o = _dot(d.reshape(N_SAMPLE, POOL_GROUP).astype(_BF16), w_ref[gi])
            out_ref[0:N_SAMPLE, sl] = x[:, sl] + o * sc_ref[:, sl]
        out_ref[N_SAMPLE:, :] = jnp.zeros((ROW_TILE - N_SAMPLE, D_MODEL), _F32)


def _pool_layer(layer, h_prompt, h_sample, sample_block, state_pool, norm_a, w_pool, pool_scale):
    halo_blocks_per_tile = ROW_TILE // HALO
    return pl.pallas_call(
        _pool_kernel,
        grid=(N_ROW_TILES,),
        in_specs=[
            pl.BlockSpec((ROW_TILE, D_MODEL), lambda i: (jnp.minimum(i, N_PROMPT_TILES - 1), 0)),
            pl.BlockSpec((HALO, D_MODEL),
                         lambda i: (jnp.clip(i * halo_blocks_per_tile - 1, 0,
                                             N_PROMPT // HALO - 1), 0)),
            pl.BlockSpec((N_SAMPLE, D_MODEL), lambda i: (sample_block, 0)),
            pl.BlockSpec((None, DEC_BATCH, POOL_STATE, D_MODEL), lambda i: (layer, 0, 0, 0),
                         pipeline_mode=pl.Buffered(1)),
            pl.BlockSpec((None, 1, D_MODEL), lambda i: (layer, 0, 0)),
            pl.BlockSpec((None, len(POOL_WINDOWS), POOL_GROUP, POOL_GROUP),
                         lambda i: (layer, 0, 0, 0)),
            pl.BlockSpec((None, 1, D_MODEL), lambda i: (layer, 0, 0)),
            pl.BlockSpec((len(MXU_POOL_WINDOWS), ROW_TILE, 2 * (MXU_HALO + ROW_TILE)),
                         lambda i: (0, 0, 0)),
        ],
        out_specs=[
            pl.BlockSpec((ROW_TILE, D_MODEL), lambda i: (i, 0)),
            pl.BlockSpec((1, POOL_STATE, D_MODEL),
                         lambda i: (jnp.minimum(i // TILES_PER_SEQ, BATCH - 1), 0, 0)),
            pl.BlockSpec((DEC_BATCH, POOL_STATE, D_MODEL), lambda i: (0, 0, 0)),
        ],
        out_shape=[
            jax.ShapeDtypeStruct((N_ROWS, D_MODEL), _F32),
            jax.ShapeDtypeStruct((BATCH, POOL_STATE, D_MODEL), _F32),
            jax.ShapeDtypeStruct((DEC_BATCH, POOL_STATE, D_MODEL), _F32),
        ],
        scratch_shapes=[
            pltpu.VMEM((MXU_HALO + ROW_TILE, D_MODEL), _F32),
            pltpu.VMEM((DEC_BATCH, HALO + DEC_SEQ, D_MODEL), _F32),
        ],
        compiler_params=pltpu.CompilerParams(
            dimension_semantics=("arbitrary",), vmem_limit_bytes=VMEM_LIMIT),
        name="pool_layer",
    )(h_prompt, h_prompt, h_sample, state_pool, norm_a, w_pool, pool_scale, _pool_band_matrices())


def _mlp_first_kernel(h_ref, g_ref, wup_ref, wdn_ref, *rest, final):
    if final:
        gf_ref, out_ref, wub_ref, wdb_ref, hn_ref = rest
    else:
        out_ref, wub_ref, wdb_ref, hn_ref = rest
    f = pl.program_id(0)

    @pl.when(f == 0)
    def _init():
        x = h_ref[...]
        hn_ref[...] = _rms(x, g_ref[...]).astype(_BF16)
        out_ref[...] = x

    wu = wup_ref[...].astype(_BF16)
    wd = wdn_ref[...].astype(_BF16)
    wub_ref[...] = wu
    wdb_ref[...] = wd
    a = jnp.maximum(_dot(hn_ref[...], wu), 0.0)
    out_ref[...] += _dot((a * a).astype(_BF16), wd)

    if final:
        @pl.when(f == pl.num_programs(0) - 1)
        def _finish():
            out_ref[...] = _rms(out_ref[...], gf_ref[...])


def _mlp_first_tile(layer, h, norm_mlp, w_up, w_down, norm_f=None):
    final = norm_f is not None
    in_specs = [
        pl.BlockSpec((MLP_ROW_TILE, D_MODEL), lambda f: (0, 0), pipeline_mode=pl.Buffered(1)),
        pl.BlockSpec((None, 1, D_MODEL), lambda f: (layer, 0, 0)),
        pl.BlockSpec((None, D_MODEL, FIRST_FF_TILE), lambda f: (layer, 0, f)),
        pl.BlockSpec((None, FIRST_FF_TILE, D_MODEL), lambda f: (layer, f, 0)),
    ]
    args = [h, norm_mlp, w_up, w_down]
    if final:
        in_specs.append(pl.BlockSpec((1, D_MODEL), lambda f: (0, 0)))
        args.append(norm_f)
    return pl.pallas_call(
        functools.partial(_mlp_first_kernel, final=final),
        grid=(D_FF // FIRST_FF_TILE,),
        in_specs=in_specs,
        out_specs=[
            pl.BlockSpec((MLP_ROW_TILE, D_MODEL), lambda f: (0, 0)),
            pl.BlockSpec((None, D_MODEL, FIRST_FF_TILE), lambda f: (f, 0, 0)),
            pl.BlockSpec((FIRST_FF_TILE, D_MODEL), lambda f: (f, 0)),
        ],
        out_shape=[
            jax.ShapeDtypeStruct((N_PROMPT if final else N_ROWS, D_MODEL), _F32),
            jax.ShapeDtypeStruct((D_FF // FIRST_FF_TILE, D_MODEL, FIRST_FF_TILE), _BF16),
            jax.ShapeDtypeStruct((D_FF, D_MODEL), _BF16),
        ],
        scratch_shapes=[pltpu.VMEM((MLP_ROW_TILE, D_MODEL), _BF16)],
        compiler_params=pltpu.CompilerParams(
            dimension_semantics=("arbitrary",), vmem_limit_bytes=VMEM_LIMIT),
        name="mlp_first_tile",
    )(*args)


def _mlp_rest_kernel(*refs, final, n_convert):
    h_ref, g_ref, wup_ref, wdn_ref = refs[1:5]
    n_in = 5 + int(final) + n_convert
    gf_ref = refs[5] if final else None
    convert_src = refs[n_in - n_convert:n_in]
    out_ref = refs[n_in]
    outs_ref = refs[n_in + 1] if final else None
    n_out = 1 + int(final)
    convert_dst = refs[n_in + n_out:n_in + n_out + n_convert]
    hn_ref = refs[n_in + n_out + n_convert]
    i = pl.program_id(0)
    f = pl.program_id(1)

    @pl.when(f == 0)
    def _init():
        x = h_ref[...]
        hn_ref[...] = _rms(x, g_ref[...]).astype(_BF16)
        out_ref[...] = x

    acc = None
    for k in range(FF_SUBBLOCKS):
        a = jnp.maximum(_dot(hn_ref[...], wup_ref[k]), 0.0)
        part = _dot((a * a).astype(_BF16), wdn_ref[k * FIRST_FF_TILE:(k + 1) * FIRST_FF_TILE, :])
        acc = part if acc is None else acc + part
    out_ref[...] += acc

    if n_convert:
        @pl.when(i * pl.num_programs(1) + f < CONVERT_STEPS)
        def _convert():
            for src, dst in zip(convert_src, convert_dst):
                dst[...] = src[...].astype(_BF16)

    if final:
        last_f = f == pl.num_programs(1) - 1

        @pl.when(last_f)
        def _finish():
            out_ref[...] = _rms(out_ref[...], gf_ref[...])

        @pl.when(last_f & (i == pl.num_programs(0) - 1))
        def _split():
            outs_ref[...] = out_ref[MLP_SAMPLE_ROW0:, :]


def _mlp_rest_tiles(layer, h, first, norm_mlp, w_up, w_down, norm_f=None, convert=()):
    final = norm_f is not None
    n_ff = D_FF // MLP_FF_TILE
    assert (N_MLP_TILES - 1) * n_ff >= CONVERT_STEPS
    in_specs = [
        pl.BlockSpec(memory_space=pl.ANY),
        pl.BlockSpec((MLP_ROW_TILE, D_MODEL), lambda i, f: (i + 1, 0)),
        pl.BlockSpec((None, 1, D_MODEL), lambda i, f: (layer, 0, 0)),
        pl.BlockSpec((FF_SUBBLOCKS, D_MODEL, FIRST_FF_TILE), lambda i, f: (f, 0, 0)),
        pl.BlockSpec((MLP_FF_TILE, D_MODEL), lambda i, f: (f, 0)),
    ]
    args = [first, h, norm_mlp, w_up, w_down]
    row_spec = pl.BlockSpec((MLP_ROW_TILE, D_MODEL), lambda i, f: (i + 1, 0))
    if final:
        in_specs.append(pl.BlockSpec((1, D_MODEL), lambda i, f: (0, 0)))
        args.append(norm_f)
        out_specs = [row_spec, pl.BlockSpec((N_SAMPLE, D_MODEL), lambda i, f: (0, 0))]
        out_shape = [jax.ShapeDtypeStruct((N_PROMPT, D_MODEL), _F32),
                     jax.ShapeDtypeStruct((N_SAMPLE, D_MODEL), _F32)]
    else:
        out_specs = [row_spec]
        out_shape = [jax.ShapeDtypeStruct((N_ROWS, D_MODEL), _F32)]
    band = lambda i, f: jnp.minimum(i * n_ff + f, CONVERT_STEPS - 1)
    for stacked, idx in convert:
        _, rows, cols = stacked.shape
        rb = rows // CONVERT_STEPS
        assert rb * CONVERT_STEPS == rows and rb % 16 == 0
        in_specs.append(pl.BlockSpec((None, rb, cols), lambda i, f, idx=idx: (idx, band(i, f), 0)))
        out_specs.append(pl.BlockSpec((rb, cols), lambda i, f: (band(i, f), 0)))
        out_shape.append(jax.ShapeDtypeStruct((rows, cols), _BF16))
        args.append(stacked)
    return pl.pallas_call(
        functools.partial(_mlp_rest_kernel, final=final, n_convert=len(convert)),
        grid=(N_MLP_TILES - 1, n_ff),
        in_specs=in_specs,
        out_specs=out_specs,
        out_shape=out_shape,
        input_output_aliases={0: 0},
        scratch_shapes=[pltpu.VMEM((MLP_ROW_TILE, D_MODEL), _BF16)],
        compiler_params=pltpu.CompilerParams(
            dimension_semantics=("arbitrary", "arbitrary"), vmem_limit_bytes=VMEM_LIMIT),
        name="mlp_rest_tiles",
    )(*args)


def _mlp_layer(layer, h, norm_mlp, w_up, w_down, norm_f=None, convert=()):
    first, w_up_b, w_down_b = _mlp_first_tile(layer, h, norm_mlp, w_up, w_down, norm_f)
    return _mlp_rest_tiles(layer, h, first, norm_mlp, w_up_b, w_down_b, norm_f, convert)


def _kv_kernel(h_ref, g_ref, w_ref, out_ref):
    hk = _rms(h_ref[...], g_ref[...]).astype(_BF16)
    out_ref[...] = _dot(hk, w_ref[...])


def _kv_proj(h, gain, w_kv):
    return pl.pallas_call(
        _kv_kernel,
        grid=(N_MLP_TILES,),
        in_specs=[
            pl.BlockSpec((MLP_ROW_TILE, D_MODEL), lambda i: (i, 0)),
            pl.BlockSpec((1, D_MODEL), lambda i: (0, 0)),
            pl.BlockSpec((D_MODEL, 2 * KV_DIM), lambda i: (0, 0)),
        ],
        out_specs=pl.BlockSpec((MLP_ROW_TILE, 2 * KV_DIM), lambda i: (i, 0)),
        out_shape=jax.ShapeDtypeStruct((N_ROWS, 2 * KV_DIM), _F32),
        compiler_params=pltpu.CompilerParams(
            dimension_semantics=("arbitrary",), vmem_limit_bytes=VMEM_LIMIT),
        name="kv_proj",
    )(h, gain, w_kv)


def _pair_diag_tall(x_full, g):
    keys = x_full.shape[0]
    c, e0 = divmod(g, 2)
    xa = x_full[:, c * 2 * HEAD_DIM:(c + 1) * 2 * HEAD_DIM]
    xr = pltpu.roll(xa, HEAD_DIM, 1)
    low_half = lax.broadcasted_iota(jnp.int32, (keys, 2 * HEAD_DIM), 1) < HEAD_DIM
    top = jnp.where(low_half, xa if e0 == 0 else xr, 0.0)
    bot = jnp.where(low_half, 0.0, xr if e0 == 0 else xa)
    return jnp.concatenate([top, bot], axis=0).astype(_BF16)


def _pair_diag_wide(xt, g):
    xg = xt[g * HEAD_DIM:(g + 1) * HEAD_DIM, :]
    z = jnp.zeros_like(xg)
    return jnp.concatenate([jnp.concatenate([xg, z], axis=1),
                            jnp.concatenate([z, xg], axis=1)], axis=0)


def _rows_scores(q_groups, k_full):
    kt = k_full.T.astype(_BF16)
    scores = []
    for g in range(N_KV_HEADS):
        qg = q_groups[g]
        qs = jnp.concatenate([qg[:, p * 128:(p + 1) * 128] for p in range(4)], axis=0)
        scores.append(_dot(qs.astype(_BF16), _pair_diag_wide(kt, g)))
    return scores


def _rows_softmax(scores, allowed, tq, sink_ref):
    probs = []
    for g in range(N_KV_HEADS):
        s = scores[g]
        p_rows = []
        for p in range(4):
            p_cols = []
            for e in range(2):
                sink = sink_ref[g * 8 + p * 2 + e]
                spe = s[p * tq:(p + 1) * tq, e * KEY_SLOTS:(e + 1) * KEY_SLOTS]
                spe = jnp.where(allowed, spe, -jnp.inf)
                m = jnp.maximum(jnp.max(spe, axis=-1, keepdims=True), sink)
                pe = jnp.exp(spe - m)
                l = jnp.sum(pe, axis=-1, keepdims=True) + jnp.exp(sink - m)
                p_cols.append(pe * (1.0 / l))
            p_rows.append(jnp.concatenate(p_cols, axis=1))
        probs.append(jnp.concatenate(p_rows, axis=0).astype(_BF16))
    return probs


def _rows_values(probs, v_full, tq):
    outs = []
    for g in range(N_KV_HEADS):
        o = _dot(probs[g], _pair_diag_tall(v_full, g))
        outs.append(jnp.concatenate([o[p * tq:(p + 1) * tq, :] for p in range(4)], axis=1))
    return outs


def _attn_sample_rows(sink_ref, x, kv_ref, ck_ref, cv_ref, g_ref, wq_ref, wo_ref, q_ref, o_ref):
    hn = _rms(x, g_ref[...]).astype(_BF16)
    q = _dot(hn, wq_ref[...]) * (HEAD_DIM ** -0.5)
    for g in range(N_KV_HEADS):
        q_ref[g] = q[:, g * GROUP_LANES:(g + 1) * GROUP_LANES]

    qi = lax.broadcasted_iota(jnp.int32, (DEC_SEQ, KEY_SLOTS), 0)
    kj = lax.broadcasted_iota(jnp.int32, (DEC_SEQ, KEY_SLOTS), 1)
    allowed = ((kj < WINDOW) & (kj > qi)) | ((kj >= WINDOW) & (kj - WINDOW <= qi))
    pad = jnp.zeros((KEY_SLOTS - WINDOW - DEC_SEQ, KV_DIM), _F32)

    def body(j, carry):
        seqs = []
        for u in range(SAMPLE_UNROLL):
            b = j * SAMPLE_UNROLL + u
            rows = pl.ds(pl.multiple_of(b * DEC_SEQ, DEC_SEQ), DEC_SEQ)
            kb = jnp.concatenate([ck_ref[b], kv_ref[rows, 0:KV_DIM], pad], axis=0)
            scores = _rows_scores([q_ref[g, rows, :] for g in range(N_KV_HEADS)], kb)
            seqs.append((b, rows, scores))
        probs = [_rows_softmax(scores, allowed, DEC_SEQ, sink_ref) for _, _, scores in seqs]
        for (b, rows, _), pm in zip(seqs, probs):
            vb = jnp.concatenate([cv_ref[b], kv_ref[rows, KV_DIM:2 * KV_DIM], pad], axis=0)
            outs = _rows_values(pm, vb, DEC_SEQ)
            for g in range(N_KV_HEADS):
                o_ref[g, rows, :] = outs[g]
        return carry

    lax.fori_loop(0, DEC_BATCH // SAMPLE_UNROLL, body, 0)

    acc = x
    for g in range(N_KV_HEADS):
        acc = acc + _dot(o_ref[g].astype(_BF16), wo_ref[g * GROUP_LANES:(g + 1) * GROUP_LANES, :])
    return acc


def _attn_core_lanes(qt_ref, ot_ref, cols, k_full, v_full, allowed_t, sink_ref, between):
    vt = v_full.T.astype(_BF16)

    def scores(g):
        base = g * GROUP_LANES
        qst = jnp.concatenate([qt_ref[base + p * 128:base + (p + 1) * 128, cols]
                               for p in range(4)], axis=1)
        return _dot(_pair_diag_tall(k_full, g), qst)

    def softmax(g, st):
        inv = {}
        pt_rows = []
        for e in range(2):
            pt_cols = []
            for p in range(4):
                sink = sink_ref[g * 8 + p * 2 + e]
                sub = st[e * KEY_SLOTS:(e + 1) * KEY_SLOTS, p * WINDOW:(p + 1) * WINDOW]
                sub = jnp.where(allowed_t, sub, -jnp.inf)
                m = jnp.maximum(jnp.max(sub, axis=0, keepdims=True), sink)
                pe = jnp.exp(sub - m)
                l = jnp.sum(pe, axis=0, keepdims=True) + jnp.exp(sink - m)
                inv[e, p] = 1.0 / l
                pt_cols.append(pe.astype(_BF16))
            pt_rows.append(jnp.concatenate(pt_cols, axis=1))
        return jnp.concatenate(pt_rows, axis=0), inv

    def values(g, pt, inv):
        base = g * GROUP_LANES
        ot = _dot(_pair_diag_wide(vt, g), pt)
        for e in range(2):
            for p in range(4):
                r0 = base + p * 128 + e * HEAD_DIM
                ot_ref[r0:r0 + HEAD_DIM, cols] = (
                    ot[e * HEAD_DIM:(e + 1) * HEAD_DIM, p * WINDOW:(p + 1) * WINDOW] * inv[e, p])

    st = {0: scores(0), 1: scores(1)}
    between()
    for g in range(N_KV_HEADS):
        pt, inv = softmax(g, st.pop(g))
        if g + 2 < N_KV_HEADS:
            st[g + 2] = scores(g + 2)
        values(g, pt, inv)


def _attn_kernel(sink_ref, h_ref, kv_ref, kvp_ref, ck_ref, cv_ref, g_ref, wq_ref, wo_ref,
                 out_ref, qt_ref, ot_ref, q_ref, o_ref):
    i = pl.program_id(0)

    @pl.when(i < N_PROMPT_TILES)
    def _prompt():
        n_blocks = ROW_TILE // WINDOW
        cols = [slice(n * WINDOW, (n + 1) * WINDOW) for n in range(n_blocks)]

        def project_q(n):
            hn = _rms(h_ref[cols[n], :], g_ref[...]).astype(_BF16)
            q = _dot(hn, wq_ref[...]) * (HEAD_DIM ** -0.5)
            qt_ref[:, cols[n]] = q.T.astype(_BF16)

        def project_o(n):
            o = ot_ref[:, cols[n]].T.astype(_BF16)
            out_ref[cols[n], :] = h_ref[cols[n], :] + _dot(o, wo_ref[...])

        def between_blocks(n):
            if n > 0:
                project_o(n - 1)
            if n + 1 < n_blocks:
                project_q(n + 1)

        first = (i % TILES_PER_SEQ) == 0
        kj = lax.broadcasted_iota(jnp.int32, (KEY_SLOTS, WINDOW), 0)
        qi = lax.broadcasted_iota(jnp.int32, (KEY_SLOTS, WINDOW), 1)
        band = (kj > qi) & (kj <= qi + WINDOW)
        project_q(0)
        for n in range(n_blocks):
            allowed_t = band
            if n == 0:
                allowed_t = band & ((kj >= WINDOW) | jnp.logical_not(first))
                kv_prev = kvp_ref[...]
            else:
                kv_prev = kv_ref[(n - 1) * WINDOW:n * WINDOW, :]
            kv_blk = jnp.concatenate([kv_prev, kv_ref[cols[n], :]], axis=0)
            _attn_core_lanes(qt_ref, ot_ref, cols[n], kv_blk[:, 0:KV_DIM], kv_blk[:, KV_DIM:],
                             allowed_t, sink_ref, functools.partial(between_blocks, n))
        project_o(n_blocks - 1)

    @pl.when(i == N_PROMPT_TILES)
    def _sample():
        out_ref[0:N_SAMPLE, :] = _attn_sample_rows(
            sink_ref, h_ref[0:N_SAMPLE, :], kv_ref, ck_ref, cv_ref, g_ref, wq_ref, wo_ref, q_ref, o_ref)
        out_ref[N_SAMPLE:, :] = jnp.zeros((ROW_TILE - N_SAMPLE, D_MODEL), _F32)


def _attn_layer(layer, h, kv, cache_k, cache_v, norm_b, w_q, w_o, sinks):
    kv_blocks_per_tile = ROW_TILE // WINDOW
    const1 = pl.Buffered(1)
    return pl.pallas_call(
        _attn_kernel,
        grid=(N_ROW_TILES,),
        in_specs=[
            pl.BlockSpec(memory_space=pltpu.SMEM),
            pl.BlockSpec((ROW_TILE, D_MODEL), lambda i: (i, 0)),
            pl.BlockSpec((ROW_TILE, 2 * KV_DIM), lambda i: (i, 0)),
            pl.BlockSpec((WINDOW, 2 * KV_DIM),
                         lambda i: (jnp.maximum(i * kv_blocks_per_tile - 1, 0), 0)),
            pl.BlockSpec((DEC_BATCH, WINDOW, KV_DIM), lambda i: (0, 0, 0), pipeline_mode=const1),
            pl.BlockSpec((DEC_BATCH, WINDOW, KV_DIM), lambda i: (0, 0, 0), pipeline_mode=const1),
            pl.BlockSpec((None, 1, D_MODEL), lambda i: (layer, 0, 0)),
            pl.BlockSpec((D_MODEL, D_MODEL), lambda i: (0, 0), pipeline_mode=const1),
            pl.BlockSpec((D_MODEL, D_MODEL), lambda i: (0, 0), pipeline_mode=const1),
        ],
        out_specs=pl.BlockSpec((ROW_TILE, D_MODEL), lambda i: (i, 0)),
        out_shape=jax.ShapeDtypeStruct((N_ROWS, D_MODEL), _F32),
        scratch_shapes=[
            pltpu.VMEM((D_MODEL, ROW_TILE), _BF16),
            pltpu.VMEM((D_MODEL, ROW_TILE), _F32),
            pltpu.VMEM((N_KV_HEADS, N_SAMPLE, GROUP_LANES), _F32),
            pltpu.VMEM((N_KV_HEADS, N_SAMPLE, GROUP_LANES), _F32),
        ],
        compiler_params=pltpu.CompilerParams(
            dimension_semantics=("arbitrary",), vmem_limit_bytes=VMEM_LIMIT),
        name="attn_layer",
    )(sinks, h, kv, kv, cache_k, cache_v, norm_b, w_q, w_o)


def kernel(x_prompt, x_sample, state_pool, cache_k_win, cache_v_win, norm_a, w_pool, pool_scale,
           norm_kv, w_k, w_v, norm_b, w_q, w_o, sinks, norm_mlp, w_up, w_down, norm_f):
    cache_k = cache_k_win.reshape(DEC_BATCH, WINDOW, KV_DIM)
    cache_v = cache_v_win.reshape(DEC_BATCH, WINDOW, KV_DIM)
    rows3 = lambda v: v.reshape(v.shape[0], 1, D_MODEL)
    norm_a3, scale3, norm_b3, norm_mlp3 = rows3(norm_a), rows3(pool_scale), rows3(norm_b), rows3(norm_mlp)
    w_pool_b = w_pool.astype(_BF16)
    w_kv = jnp.concatenate([w_k, w_v], axis=1).astype(_BF16)
    w_q_b = w_o_b = kv = None

    pool_p, pool_s = [], []
    h_prompt = x_prompt.reshape(N_PROMPT, D_MODEL)
    h_sample, sample_block = x_sample.reshape(N_SAMPLE, D_MODEL), 0
    for l in range(DEPTH):
        if l < N_A_LAYERS:
            h, pp, ps = _pool_layer(l, h_prompt, h_sample, sample_block, state_pool, norm_a3,
                                    w_pool_b, scale3)
            pool_p.append(pp)
            pool_s.append(ps)
        else:
            h = _attn_layer(l - N_A_LAYERS, h, kv, cache_k, cache_v, norm_b3, w_q_b, w_o_b,
                            sinks[l - N_A_LAYERS])
        if l == DEPTH - 1:
            y_prompt, y_sample = _mlp_layer(l, h, norm_mlp3, w_up, w_down,
                                            norm_f=norm_f.reshape(1, D_MODEL))
            break
        j = l + 1 - N_A_LAYERS
        convert = [(w_q, j), (w_o, j)] if j >= 0 else []
        h, *w_attn = _mlp_layer(l, h, norm_mlp3, w_up, w_down, convert=convert)
        if w_attn:
            w_q_b, w_o_b = w_attn
        h_prompt, h_sample, sample_block = h, h, SAMPLE_BLOCK
        if l == N_A_LAYERS - 1:
            kv = _kv_proj(h, norm_kv.reshape(1, D_MODEL), w_kv)

    kv_tail = jnp.stack([kv[(b + 1) * SEQ - WINDOW:(b + 1) * SEQ] for b in range(BATCH)])
    kv_new = kv[N_PROMPT:]

    def prompt_window(a):
        return a.reshape(BATCH, WINDOW, N_KV_HEADS, HEAD_DIM)

    def sample_window(buf, a):
        new = a.reshape(DEC_BATCH, DEC_SEQ, N_KV_HEADS, HEAD_DIM)
        return jnp.concatenate([buf[:, DEC_SEQ:], new], axis=1)

    return (y_prompt.reshape(BATCH, SEQ, D_MODEL), y_sample.reshape(DEC_BATCH, DEC_SEQ, D_MODEL),
            jnp.stack(pool_p), jnp.stack(pool_s),
            prompt_window(kv_tail[..., :KV_DIM]), prompt_window(kv_tail[..., KV_DIM:]),
            sample_window(cache_k_win, kv_new[:, :KV_DIM]), sample_window(cache_v_win, kv_new[:, KV_DIM:]))
```

```python
import functools

import jax
import jax.numpy as jnp
import numpy as np
from jax import lax
from jax.experimental import pallas as pl
from jax.experimental.pallas import tpu as pltpu

D_MODEL = 2048
BATCH = 4
SEQ = 2048
DEC_BATCH = 32
DEC_SEQ = 8
N_A_LAYERS = 2
N_B_LAYERS = 2
DEPTH = 4
POOL_WINDOWS = (2, 4, 8, 16)
POOL_GROUP = D_MODEL // len(POOL_WINDOWS)
POOL_STATE = 15
HEAD_DIM = 64
N_HEADS = 32
N_KV_HEADS = 4
KV_DIM = N_KV_HEADS * HEAD_DIM
GROUP_LANES = D_MODEL // N_KV_HEADS
WINDOW = 128
D_FF = 4 * D_MODEL
RMS_EPS = 1e-5

N_PROMPT = BATCH * SEQ
N_SAMPLE = DEC_BATCH * DEC_SEQ
N_ROWS = N_PROMPT + N_SAMPLE

ROW_TILE = 512
N_ROW_TILES = -(-N_ROWS // ROW_TILE)
TILES_PER_SEQ = SEQ // ROW_TILE
N_PROMPT_TILES = N_PROMPT // ROW_TILE
SAMPLE_BLOCK = N_PROMPT // N_SAMPLE
HALO = 16
MXU_HALO = 128
MXU_POOL_WINDOWS = (8, 16)
MLP_ROW_TILE = 768
MLP_FF_TILE = 1024
FIRST_FF_TILE = 512
FF_SUBBLOCKS = MLP_FF_TILE // FIRST_FF_TILE
N_MLP_TILES = N_ROWS // MLP_ROW_TILE
MLP_SAMPLE_ROW0 = N_PROMPT - (N_MLP_TILES - 1) * MLP_ROW_TILE
assert N_MLP_TILES * MLP_ROW_TILE == N_ROWS and MLP_SAMPLE_ROW0 + N_SAMPLE == MLP_ROW_TILE
KEY_SLOTS = 2 * WINDOW
SAMPLE_UNROLL = 4
CONVERT_STEPS = 64

VMEM_LIMIT = 56 * 1024 * 1024

_BF16 = jnp.bfloat16
_F32 = jnp.float32


def _rms(x, g):
    ms = jnp.mean(x * x, axis=-1, keepdims=True)
    return (x * lax.rsqrt(ms + RMS_EPS)) * g


def _dot(a, b):
    return jnp.dot(a, b, preferred_element_type=_F32)


def _pool_band_matrices():
    t = np.arange(ROW_TILE)[:, None]
    r = np.arange(MXU_HALO + ROW_TILE)[None, :]
    mats = []
    for w in MXU_POOL_WINDOWS:
        band = ((r > t + MXU_HALO - w) & (r <= t + MXU_HALO)).astype(np.float32)
        mats.append(np.concatenate([band, band], axis=1))
    return jnp.asarray(np.stack(mats), dtype=_BF16)


def _pool_kernel(hp_ref, halo_ref, hs_ref, past_ref, g_ref, w_ref, sc_ref, band_ref,
                 out_ref, poolp_ref, pools_ref, ext_ref, ext3_ref):
    i = pl.program_id(0)
    g = g_ref[...]

    @pl.when(i == 0)
    def _zero_unused_lookback():
        ext_ref[0:MXU_HALO - HALO, :] = jnp.zeros((MXU_HALO - HALO, D_MODEL), _F32)

    @pl.when(i < N_PROMPT_TILES)
    def _prompt():
        x = hp_ref[...]
        u = _rms(x, g)
        first = (i % TILES_PER_SEQ) == 0
        halo = jnp.where(first, 0.0, _rms(halo_ref[...], g))
        ext_ref[MXU_HALO - HALO:MXU_HALO, :] = halo
        ext_ref[MXU_HALO:MXU_HALO + ROW_TILE, :] = u
        poolp_ref[0] = ext_ref[MXU_HALO + ROW_TILE - POOL_STATE:MXU_HALO + ROW_TILE, :]
        pos = (i % TILES_PER_SEQ) * ROW_TILE + lax.broadcasted_iota(jnp.int32, (ROW_TILE, 1), 0)
        groups = [slice(gi * POOL_GROUP, (gi + 1) * POOL_GROUP) for gi in range(len(POOL_WINDOWS))]
        mxu_sums = {}
        for k, w in enumerate(MXU_POOL_WINDOWS):
            e = ext_ref[:, groups[POOL_WINDOWS.index(w)]]
            hi = e.astype(_BF16)
            lo = (e - hi.astype(_F32)).astype(_BF16)
            mxu_sums[w] = _dot(band_ref[k], jnp.concatenate([hi, lo], axis=0))
        for gi, w in enumerate(POOL_WINDOWS):
            sl = groups[gi]
            ug = u[:, sl]
            if w in mxu_sums:
                s = mxu_sums[w]
            else:
                s = ug
                for j in range(1, w):
                    s = s + ext_ref[MXU_HALO - j:MXU_HALO - j + ROW_TILE, sl]
            inv = 1.0 / jnp.minimum(pos + 1, w).astype(_F32)
            d = s * inv - ug
            o = _dot(d.astype(_BF16), w_ref[gi])
            out_ref[:, sl] = x[:, sl] + o * sc_ref[:, sl]

    @pl.when(i == N_PROMPT_TILES)
    def _sample():
        x = hs_ref[...]
        u = _rms(x, g)
        u3 = u.reshape(DEC_BATCH, DEC_SEQ, D_MODEL)
        ext3_ref[:, HALO - POOL_STATE:HALO, :] = past_ref[...]
        ext3_ref[:, HALO:HALO + DEC_SEQ, :] = u3
        pools_ref[:, 0:POOL_STATE - DEC_SEQ, :] = past_ref[:, DEC_SEQ:POOL_STATE, :]
        pools_ref[:, POOL_STATE - DEC_SEQ:POOL_STATE, :] = u3
        for gi, w in enumerate(POOL_WINDOWS):
            sl = slice(gi * POOL_GROUP, (gi + 1) * POOL_GROUP)
            ug = u3[:, :, sl]
            s = ug
            for j in range(1, w):
                s = s + ext3_ref[:, HALO - j:HALO - j + DEC_SEQ, sl]
            d = s * (1.0 / w) - ug
            o = _dot(d.reshape(N_SAMPLE, POOL_GROUP).astype(_BF16), w_ref[gi])
            out_ref[0:N_SAMPLE, sl] = x[:, sl] + o * sc_ref[:, sl]
        out_ref[N_SAMPLE:, :] = jnp.zeros((ROW_TILE - N_SAMPLE, D_MODEL), _F32)


def _pool_layer(layer, h_prompt, h_sample, sample_block, state_pool, norm_a, w_pool, pool_scale):
    halo_blocks_per_tile = ROW_TILE // HALO
    return pl.pallas_call(
        _pool_kernel,
        grid=(N_ROW_TILES,),
        in_specs=[
            pl.BlockSpec((ROW_TILE, D_MODEL), lambda i: (jnp.minimum(i, N_PROMPT_TILES - 1), 0)),
            pl.BlockSpec((HALO, D_MODEL),
                         lambda i: (jnp.clip(i * halo_blocks_per_tile - 1, 0,
                                             N_PROMPT // HALO - 1), 0)),
            pl.BlockSpec((N_SAMPLE, D_MODEL), lambda i: (sample_block, 0)),
            pl.BlockSpec((None, DEC_BATCH, POOL_STATE, D_MODEL), lambda i: (layer, 0, 0, 0),
                         pipeline_mode=pl.Buffered(1)),
            pl.BlockSpec((None, 1, D_MODEL), lambda i: (layer, 0, 0)),
            pl.BlockSpec((None, len(POOL_WINDOWS), POOL_GROUP, POOL_GROUP),
                         lambda i: (layer, 0, 0, 0)),
            pl.BlockSpec((None, 1, D_MODEL), lambda i: (layer, 0, 0)),
            pl.BlockSpec((len(MXU_POOL_WINDOWS), ROW_TILE, 2 * (MXU_HALO + ROW_TILE)),
                         lambda i: (0, 0, 0)),
        ],
        out_specs=[
            pl.BlockSpec((ROW_TILE, D_MODEL), lambda i: (i, 0)),
            pl.BlockSpec((1, POOL_STATE, D_MODEL),
                         lambda i: (jnp.minimum(i // TILES_PER_SEQ, BATCH - 1), 0, 0)),
            pl.BlockSpec((DEC_BATCH, POOL_STATE, D_MODEL), lambda i: (0, 0, 0)),
        ],
        out_shape=[
            jax.ShapeDtypeStruct((N_ROWS, D_MODEL), _F32),
            jax.ShapeDtypeStruct((BATCH, POOL_STATE, D_MODEL), _F32),
            jax.ShapeDtypeStruct((DEC_BATCH, POOL_STATE, D_MODEL), _F32),
        ],
        scratch_shapes=[
            pltpu.VMEM((MXU_HALO + ROW_TILE, D_MODEL), _F32),
            pltpu.VMEM((DEC_BATCH, HALO + DEC_SEQ, D_MODEL), _F32),
        ],
        compiler_params=pltpu.CompilerParams(
            dimension_semantics=("arbitrary",), vmem_limit_bytes=VMEM_LIMIT),
        name="pool_layer",
    )(h_prompt, h_prompt, h_sample, state_pool, norm_a, w_pool, pool_scale, _pool_band_matrices())


def _mlp_first_kernel(h_ref, g_ref, wup_ref, wdn_ref, *rest, final):
    if final:
        gf_ref, out_ref, wub_ref, wdb_ref, hn_ref, wu_buf, wd_buf = rest
    else:
        out_ref, wub_ref, wdb_ref, hn_ref, wu_buf, wd_buf = rest
    s = pl.program_id(0)

    @pl.when(s == 0)
    def _init():
        x = h_ref[...]
        hn_ref[...] = _rms(x, g_ref[...]).astype(_BF16)
        out_ref[...] = x
        wu_buf[1] = jnp.zeros(wu_buf.shape[1:], _BF16)
        wd_buf[1] = jnp.zeros(wd_buf.shape[1:], _BF16)

    cur = s % 2
    prev = 1 - cur
    a = jnp.maximum(_dot(hn_ref[...], wu_buf[prev]), 0.0)
    out_ref[...] += _dot((a * a).astype(_BF16), wd_buf[prev])
    wu = wup_ref[...].astype(_BF16)
    wd = wdn_ref[...].astype(_BF16)
    wub_ref[...] = wu
    wdb_ref[...] = wd
    wu_buf[cur] = wu
    wd_buf[cur] = wd

    if final:
        @pl.when(s == pl.num_programs(0) - 1)
        def _finish():
            out_ref[...] = _rms(out_ref[...], gf_ref[...])


def _mlp_first_tile(layer, h, norm_mlp, w_up, w_down, norm_f=None):
    final = norm_f is not None
    n_blocks = D_FF // FIRST_FF_TILE
    blk = lambda s: jnp.minimum(s, n_blocks - 1)
    in_specs = [
        pl.BlockSpec((MLP_ROW_TILE, D_MODEL), lambda s: (0, 0), pipeline_mode=pl.Buffered(1)),
        pl.BlockSpec((None, 1, D_MODEL), lambda s: (layer, 0, 0)),
        pl.BlockSpec((None, D_MODEL, FIRST_FF_TILE), lambda s: (layer, 0, blk(s))),
        pl.BlockSpec((None, FIRST_FF_TILE, D_MODEL), lambda s: (layer, blk(s), 0)),
    ]
    args = [h, norm_mlp, w_up, w_down]
    if final:
        in_specs.append(pl.BlockSpec((1, D_MODEL), lambda s: (0, 0)))
        args.append(norm_f)
    return pl.pallas_call(
        functools.partial(_mlp_first_kernel, final=final),
        grid=(n_blocks + 1,),
        in_specs=in_specs,
        out_specs=[
            pl.BlockSpec((MLP_ROW_TILE, D_MODEL), lambda s: (0, 0)),
            pl.BlockSpec((None, D_MODEL, FIRST_FF_TILE), lambda s: (blk(s), 0, 0)),
            pl.BlockSpec((FIRST_FF_TILE, D_MODEL), lambda s: (blk(s), 0)),
        ],
        out_shape=[
            jax.ShapeDtypeStruct((N_PROMPT if final else N_ROWS, D_MODEL), _F32),
            jax.ShapeDtypeStruct((n_blocks, D_MODEL, FIRST_FF_TILE), _BF16),
            jax.ShapeDtypeStruct((D_FF, D_MODEL), _BF16),
        ],
        scratch_shapes=[
            pltpu.VMEM((MLP_ROW_TILE, D_MODEL), _BF16),
            pltpu.VMEM((2, D_MODEL, FIRST_FF_TILE), _BF16),
            pltpu.VMEM((2, FIRST_FF_TILE, D_MODEL), _BF16),
        ],
        compiler_params=pltpu.CompilerParams(
            dimension_semantics=("arbitrary",), vmem_limit_bytes=VMEM_LIMIT),
        name="mlp_first_tile",
    )(*args)


def _mlp_rest_kernel(*refs, final, n_convert):
    h_ref, g_ref, wup_ref, wdn_ref = refs[1:5]
    n_in = 5 + int(final) + n_convert
    gf_ref = refs[5] if final else None
    convert_src = refs[n_in - n_convert:n_in]
    out_ref = refs[n_in]
    outs_ref = refs[n_in + 1] if final else None
    n_out = 1 + int(final)
    convert_dst = refs[n_in + n_out:n_in + n_out + n_convert]
    hn_ref = refs[n_in + n_out + n_convert]
    i = pl.program_id(0)
    f = pl.program_id(1)

    @pl.when(f == 0)
    def _init():
        x = h_ref[...]
        hn_ref[...] = _rms(x, g_ref[...]).astype(_BF16)
        out_ref[...] = x

    acc = None
    for k in range(FF_SUBBLOCKS):
        a = jnp.maximum(_dot(hn_ref[...], wup_ref[k]), 0.0)
        part = _dot((a * a).astype(_BF16), wdn_ref[k * FIRST_FF_TILE:(k + 1) * FIRST_FF_TILE, :])
        acc = part if acc is None else acc + part
    out_ref[...] += acc

    if n_convert:
        @pl.when(i * pl.num_programs(1) + f < CONVERT_STEPS)
        def _convert():
            for src, dst in zip(convert_src, convert_dst):
                dst[...] = src[...].astype(_BF16)

    if final:
        last_f = f == pl.num_programs(1) - 1

        @pl.when(last_f)
        def _finish():
            out_ref[...] = _rms(out_ref[...], gf_ref[...])

        @pl.when(last_f & (i == pl.num_programs(0) - 1))
        def _split():
            outs_ref[...] = out_ref[MLP_SAMPLE_ROW0:, :]


def _mlp_rest_tiles(layer, h, first, norm_mlp, w_up, w_down, norm_f=None, convert=()):
    final = norm_f is not None
    n_ff = D_FF // MLP_FF_TILE
    assert (N_MLP_TILES - 1) * n_ff >= CONVERT_STEPS
    in_specs = [
        pl.BlockSpec(memory_space=pl.ANY),
        pl.BlockSpec((MLP_ROW_TILE, D_MODEL), lambda i, f: (i + 1, 0)),
        pl.BlockSpec((None, 1, D_MODEL), lambda i, f: (layer, 0, 0)),
        pl.BlockSpec((FF_SUBBLOCKS, D_MODEL, FIRST_FF_TILE), lambda i, f: (f, 0, 0)),
        pl.BlockSpec((MLP_FF_TILE, D_MODEL), lambda i, f: (f, 0)),
    ]
    args = [first, h, norm_mlp, w_up, w_down]
    row_spec = pl.BlockSpec((MLP_ROW_TILE, D_MODEL), lambda i, f: (i + 1, 0))
    if final:
        in_specs.append(pl.BlockSpec((1, D_MODEL), lambda i, f: (0, 0)))
        args.append(norm_f)
        out_specs = [row_spec, pl.BlockSpec((N_SAMPLE, D_MODEL), lambda i, f: (0, 0))]
        out_shape = [jax.ShapeDtypeStruct((N_PROMPT, D_MODEL), _F32),
                     jax.ShapeDtypeStruct((N_SAMPLE, D_MODEL), _F32)]
    else:
        out_specs = [row_spec]
        out_shape = [jax.ShapeDtypeStruct((N_ROWS, D_MODEL), _F32)]
    band = lambda i, f: jnp.minimum(i * n_ff + f, CONVERT_STEPS - 1)
    for stacked, idx in convert:
        _, rows, cols = stacked.shape
        rb = rows // CONVERT_STEPS
        assert rb * CONVERT_STEPS == rows and rb % 16 == 0
        in_specs.append(pl.BlockSpec((None, rb, cols), lambda i, f, idx=idx: (idx, band(i, f), 0)))
        out_specs.append(pl.BlockSpec((rb, cols), lambda i, f: (band(i, f), 0)))
        out_shape.append(jax.ShapeDtypeStruct((rows, cols), _BF16))
        args.append(stacked)
    return pl.pallas_call(
        functools.partial(_mlp_rest_kernel, final=final, n_convert=len(convert)),
        grid=(N_MLP_TILES - 1, n_ff),
        in_specs=in_specs,
        out_specs=out_specs,
        out_shape=out_shape,
        input_output_aliases={0: 0},
        scratch_shapes=[pltpu.VMEM((MLP_ROW_TILE, D_MODEL), _BF16)],
        compiler_params=pltpu.CompilerParams(
            dimension_semantics=("arbitrary", "arbitrary"), vmem_limit_bytes=VMEM_LIMIT),
        name="mlp_rest_tiles",
    )(*args)


def _mlp_layer(layer, h, norm_mlp, w_up, w_down, norm_f=None, convert=()):
    first, w_up_b, w_down_b = _mlp_first_tile(layer, h, norm_mlp, w_up, w_down, norm_f)
    return _mlp_rest_tiles(layer, h, first, norm_mlp, w_up_b, w_down_b, norm_f, convert)


def _kv_kernel(h_ref, g_ref, w_ref, out_ref):
    hk = _rms(h_ref[...], g_ref[...]).astype(_BF16)
    out_ref[...] = _dot(hk, w_ref[...])


def _kv_proj(h, gain, w_kv):
    return pl.pallas_call(
        _kv_kernel,
        grid=(N_MLP_TILES,),
        in_specs=[
            pl.BlockSpec((MLP_ROW_TILE, D_MODEL), lambda i: (i, 0)),
            pl.BlockSpec((1, D_MODEL), lambda i: (0, 0)),
            pl.BlockSpec((D_MODEL, 2 * KV_DIM), lambda i: (0, 0)),
        ],
        out_specs=pl.BlockSpec((MLP_ROW_TILE, 2 * KV_DIM), lambda i: (i, 0)),
        out_shape=jax.ShapeDtypeStruct((N_ROWS, 2 * KV_DIM), _F32),
        compiler_params=pltpu.CompilerParams(
            dimension_semantics=("arbitrary",), vmem_limit_bytes=VMEM_LIMIT),
        name="kv_proj",
    )(h, gain, w_kv)


def _pair_diag_tall(x_full, g):
    keys = x_full.shape[0]
    c, e0 = divmod(g, 2)
    xa = x_full[:, c * 2 * HEAD_DIM:(c + 1) * 2 * HEAD_DIM]
    xr = pltpu.roll(xa, HEAD_DIM, 1)
    low_half = lax.broadcasted_iota(jnp.int32, (keys, 2 * HEAD_DIM), 1) < HEAD_DIM
    top = jnp.where(low_half, xa if e0 == 0 else xr, 0.0)
    bot = jnp.where(low_half, 0.0, xr if e0 == 0 else xa)
    return jnp.concatenate([top, bot], axis=0).astype(_BF16)


def _pair_diag_wide(xt, g):
    xg = xt[g * HEAD_DIM:(g + 1) * HEAD_DIM, :]
    z = jnp.zeros_like(xg)
    return jnp.concatenate([jnp.concatenate([xg, z], axis=1),
                            jnp.concatenate([z, xg], axis=1)], axis=0)


def _rows_scores(q_groups, k_full):
    kt = k_full.T.astype(_BF16)
    scores = []
    for g in range(N_KV_HEADS):
        qg = q_groups[g]
        qs = jnp.concatenate([qg[:, p * 128:(p + 1) * 128] for p in range(4)], axis=0)
        scores.append(_dot(qs.astype(_BF16), _pair_diag_wide(kt, g)))
    return scores


def _rows_softmax(scores, allowed, tq, sink_ref):
    probs = []
    for g in range(N_KV_HEADS):
        s = scores[g]
        p_rows = []
        for p in range(4):
            p_cols = []
            for e in range(2):
                sink = sink_ref[g * 8 + p * 2 + e]
                spe = s[p * tq:(p + 1) * tq, e * KEY_SLOTS:(e + 1) * KEY_SLOTS]
                spe = jnp.where(allowed, spe, -jnp.inf)
                m = jnp.maximum(jnp.max(spe, axis=-1, keepdims=True), sink)
                pe = jnp.exp(spe - m)
                l = jnp.sum(pe, axis=-1, keepdims=True) + jnp.exp(sink - m)
                p_cols.append(pe * (1.0 / l))
            p_rows.append(jnp.concatenate(p_cols, axis=1))
        probs.append(jnp.concatenate(p_rows, axis=0).astype(_BF16))
    return probs


def _rows_values(probs, v_full, tq):
    outs = []
    for g in range(N_KV_HEADS):
        o = _dot(probs[g], _pair_diag_tall(v_full, g))
        outs.append(jnp.concatenate([o[p * tq:(p + 1) * tq, :] for p in range(4)], axis=1))
    return outs


def _attn_sample_rows(sink_ref, x, kv_ref, ck_ref, cv_ref, g_ref, wq_ref, wo_ref, q_ref, o_ref):
    hn = _rms(x, g_ref[...]).astype(_BF16)
    q = _dot(hn, wq_ref[...]) * (HEAD_DIM ** -0.5)
    for g in range(N_KV_HEADS):
        q_ref[g] = q[:, g * GROUP_LANES:(g + 1) * GROUP_LANES]

    qi = lax.broadcasted_iota(jnp.int32, (DEC_SEQ, KEY_SLOTS), 0)
    kj = lax.broadcasted_iota(jnp.int32, (DEC_SEQ, KEY_SLOTS), 1)
    allowed = ((kj < WINDOW) & (kj > qi)) | ((kj >= WINDOW) & (kj - WINDOW <= qi))
    pad = jnp.zeros((KEY_SLOTS - WINDOW - DEC_SEQ, KV_DIM), _F32)

    def body(j, carry):
        seqs = []
        for u in range(SAMPLE_UNROLL):
            b = j * SAMPLE_UNROLL + u
            rows = pl.ds(pl.multiple_of(b * DEC_SEQ, DEC_SEQ), DEC_SEQ)
            kb = jnp.concatenate([ck_ref[b], kv_ref[rows, 0:KV_DIM], pad], axis=0)
            scores = _rows_scores([q_ref[g, rows, :] for g in range(N_KV_HEADS)], kb)
            seqs.append((b, rows, scores))
        probs = [_rows_softmax(scores, allowed, DEC_SEQ, sink_ref) for _, _, scores in seqs]
        for (b, rows, _), pm in zip(seqs, probs):
            vb = jnp.concatenate([cv_ref[b], kv_ref[rows, KV_DIM:2 * KV_DIM], pad], axis=0)
            outs = _rows_values(pm, vb, DEC_SEQ)
            for g in range(N_KV_HEADS):
                o_ref[g, rows, :] = outs[g]
        return carry

    lax.fori_loop(0, DEC_BATCH // SAMPLE_UNROLL, body, 0)

    acc = x
    for g in range(N_KV_HEADS):
        acc = acc + _dot(o_ref[g].astype(_BF16), wo_ref[g * GROUP_LANES:(g + 1) * GROUP_LANES, :])
    return acc


def _attn_core_lanes(qt_ref, ot_ref, cols, k_full, v_full, allowed_t, sink_ref, between):
    vt = v_full.T.astype(_BF16)

    def scores(g):
        base = g * GROUP_LANES
        qst = jnp.concatenate([qt_ref[base + p * 128:base + (p + 1) * 128, cols]
                               for p in range(4)], axis=1)
        return _dot(_pair_diag_tall(k_full, g), qst)

    def softmax(g, st):
        inv = {}
        pt_rows = []
        for e in range(2):
            pt_cols = []
            for p in range(4):
                sink = sink_ref[g * 8 + p * 2 + e]
                sub = st[e * KEY_SLOTS:(e + 1) * KEY_SLOTS, p * WINDOW:(p + 1) * WINDOW]
                sub = jnp.where(allowed_t, sub, -jnp.inf)
                m = jnp.maximum(jnp.max(sub, axis=0, keepdims=True), sink)
                pe = jnp.exp(sub - m)
                l = jnp.sum(pe, axis=0, keepdims=True) + jnp.exp(sink - m)
                inv[e, p] = 1.0 / l
                pt_cols.append(pe.astype(_BF16))
            pt_rows.append(jnp.concatenate(pt_cols, axis=1))
        return jnp.concatenate(pt_rows, axis=0), inv

    def values(g, pt, inv):
        base = g * GROUP_LANES
        ot = _dot(_pair_diag_wide(vt, g), pt)
        for e in range(2):
            for p in range(4):
                r0 = base + p * 128 + e * HEAD_DIM
                ot_ref[r0:r0 + HEAD_DIM, cols] = (
                    ot[e * HEAD_DIM:(e + 1) * HEAD_DIM, p * WINDOW:(p + 1) * WINDOW] * inv[e, p])

    st = {0: scores(0), 1: scores(1)}
    between()
    for g in range(N_KV_HEADS):
        pt, inv = softmax(g, st.pop(g))
        if g + 2 < N_KV_HEADS:
            st[g + 2] = scores(g + 2)
        values(g, pt, inv)


def _attn_kernel(sink_ref, h_ref, kv_ref, kvp_ref, ck_ref, cv_ref, g_ref, wq_ref, wo_ref,
                 out_ref, qt_ref, ot_ref, q_ref, o_ref):
    i = pl.program_id(0)

    @pl.when(i < N_PROMPT_TILES)
    def _prompt():
        n_blocks = ROW_TILE // WINDOW
        cols = [slice(n * WINDOW, (n + 1) * WINDOW) for n in range(n_blocks)]

        def project_q(n):
            hn = _rms(h_ref[cols[n], :], g_ref[...]).astype(_BF16)
            q = _dot(hn, wq_ref[...]) * (HEAD_DIM ** -0.5)
            qt_ref[:, cols[n]] = q.T.astype(_BF16)

        def project_o(n):
            o = ot_ref[:, cols[n]].T.astype(_BF16)
            out_ref[cols[n], :] = h_ref[cols[n], :] + _dot(o, wo_ref[...])

        def between_blocks(n):
            if n > 0:
                project_o(n - 1)
            if n + 1 < n_blocks:
                project_q(n + 1)

        first = (i % TILES_PER_SEQ) == 0
        kj = lax.broadcasted_iota(jnp.int32, (KEY_SLOTS, WINDOW), 0)
        qi = lax.broadcasted_iota(jnp.int32, (KEY_SLOTS, WINDOW), 1)
        band = (kj > qi) & (kj <= qi + WINDOW)
        project_q(0)
        for n in range(n_blocks):
            allowed_t = band
            if n == 0:
                allowed_t = band & ((kj >= WINDOW) | jnp.logical_not(first))
                kv_prev = kvp_ref[...]
            else:
                kv_prev = kv_ref[(n - 1) * WINDOW:n * WINDOW, :]
            kv_blk = jnp.concatenate([kv_prev, kv_ref[cols[n], :]], axis=0)
            _attn_core_lanes(qt_ref, ot_ref, cols[n], kv_blk[:, 0:KV_DIM], kv_blk[:, KV_DIM:],
                             allowed_t, sink_ref, functools.partial(between_blocks, n))
        project_o(n_blocks - 1)

    @pl.when(i == N_PROMPT_TILES)
    def _sample():
        out_ref[0:N_SAMPLE, :] = _attn_sample_rows(
            sink_ref, h_ref[0:N_SAMPLE, :], kv_ref, ck_ref, cv_ref, g_ref, wq_ref, wo_ref, q_ref, o_ref)
        out_ref[N_SAMPLE:, :] = jnp.zeros((ROW_TILE - N_SAMPLE, D_MODEL), _F32)


def _attn_layer(layer, h, kv, cache_k, cache_v, norm_b, w_q, w_o, sinks):
    kv_blocks_per_tile = ROW_TILE // WINDOW
    const1 = pl.Buffered(1)
    return pl.pallas_call(
        _attn_kernel,
        grid=(N_ROW_TILES,),
        in_specs=[
            pl.BlockSpec(memory_space=pltpu.SMEM),
            pl.BlockSpec((ROW_TILE, D_MODEL), lambda i: (i, 0)),
            pl.BlockSpec((ROW_TILE, 2 * KV_DIM), lambda i: (i, 0)),
            pl.BlockSpec((WINDOW, 2 * KV_DIM),
                         lambda i: (jnp.maximum(i * kv_blocks_per_tile - 1, 0), 0)),
            pl.BlockSpec((DEC_BATCH, WINDOW, KV_DIM), lambda i: (0, 0, 0), pipeline_mode=const1),
            pl.BlockSpec((DEC_BATCH, WINDOW, KV_DIM), lambda i: (0, 0, 0), pipeline_mode=const1),
            pl.BlockSpec((None, 1, D_MODEL), lambda i: (layer, 0, 0)),
            pl.BlockSpec((D_MODEL, D_MODEL), lambda i: (0, 0), pipeline_mode=const1),
            pl.BlockSpec((D_MODEL, D_MODEL), lambda i: (0, 0), pipeline_mode=const1),
        ],
        out_specs=pl.BlockSpec((ROW_TILE, D_MODEL), lambda i: (i, 0)),
        out_shape=jax.ShapeDtypeStruct((N_ROWS, D_MODEL), _F32),
        scratch_shapes=[
            pltpu.VMEM((D_MODEL, ROW_TILE), _BF16),
            pltpu.VMEM((D_MODEL, ROW_TILE), _F32),
            pltpu.VMEM((N_KV_HEADS, N_SAMPLE, GROUP_LANES), _F32),
            pltpu.VMEM((N_KV_HEADS, N_SAMPLE, GROUP_LANES), _F32),
        ],
        compiler_params=pltpu.CompilerParams(
            dimension_semantics=("arbitrary",), vmem_limit_bytes=VMEM_LIMIT),
        name="attn_layer",
    )(sinks, h, kv, kv, cache_k, cache_v, norm_b, w_q, w_o)


def kernel(x_prompt, x_sample, state_pool, cache_k_win, cache_v_win, norm_a, w_pool, pool_scale,
           norm_kv, w_k, w_v, norm_b, w_q, w_o, sinks, norm_mlp, w_up, w_down, norm_f):
    cache_k = cache_k_win.reshape(DEC_BATCH, WINDOW, KV_DIM)
    cache_v = cache_v_win.reshape(DEC_BATCH, WINDOW, KV_DIM)
    rows3 = lambda v: v.reshape(v.shape[0], 1, D_MODEL)
    norm_a3, scale3, norm_b3, norm_mlp3 = rows3(norm_a), rows3(pool_scale), rows3(norm_b), rows3(norm_mlp)
    w_pool_b = w_pool.astype(_BF16)
    w_kv = jnp.concatenate([w_k, w_v], axis=1).astype(_BF16)
    w_q_b = w_o_b = kv = None

    pool_p, pool_s = [], []
    h_prompt = x_prompt.reshape(N_PROMPT, D_MODEL)
    h_sample, sample_block = x_sample.reshape(N_SAMPLE, D_MODEL), 0
    for l in range(DEPTH):
        if l < N_A_LAYERS:
            h, pp, ps = _pool_layer(l, h_prompt, h_sample, sample_block, state_pool, norm_a3,
                                    w_pool_b, scale3)
            pool_p.append(pp)
            pool_s.append(ps)
        else:
            h = _attn_layer(l - N_A_LAYERS, h, kv, cache_k, cache_v, norm_b3, w_q_b, w_o_b,
                            sinks[l - N_A_LAYERS])
        if l == DEPTH - 1:
            y_prompt, y_sample = _mlp_layer(l, h, norm_mlp3, w_up, w_down,
                                            norm_f=norm_f.reshape(1, D_MODEL))
            break
        j = l + 1 - N_A_LAYERS
        convert = [(w_q, j), (w_o, j)] if j >= 0 else []
        h, *w_attn = _mlp_layer(l, h, norm_mlp3, w_up, w_down, convert=convert)
        if w_attn:
            w_q_b, w_o_b = w_attn
        h_prompt, h_sample, sample_block = h, h, SAMPLE_BLOCK
        if l == N_A_LAYERS - 1:
            kv = _kv_proj(h, norm_kv.reshape(1, D_MODEL), w_kv)

    kv_tail = jnp.stack([kv[(b + 1) * SEQ - WINDOW:(b + 1) * SEQ] for b in range(BATCH)])
    kv_new = kv[N_PROMPT:]

    def prompt_window(a):
        return a.reshape(BATCH, WINDOW, N_KV_HEADS, HEAD_DIM)

    def sample_window(buf, a):
        new = a.reshape(DEC_BATCH, DEC_SEQ, N_KV_HEADS, HEAD_DIM)
        return jnp.concatenate([buf[:, DEC_SEQ:], new], axis=1)

    return (y_prompt.reshape(BATCH, SEQ, D_MODEL), y_sample.reshape(DEC_BATCH, DEC_SEQ, D_MODEL),
            jnp.stack(pool_p), jnp.stack(pool_s),
            prompt_window(kv_tail[..., :KV_DIM]), prompt_window(kv_tail[..., KV_DIM:]),
            sample_window(cache_k_win, kv_new[:, :KV_DIM]), sample_window(cache_v_win, kv_new[:, KV_DIM:]))
```

```python
import functools

import jax
import jax.numpy as jnp
import numpy as np
from jax import lax
from jax.experimental import pallas as pl
from jax.experimental.pallas import tpu as pltpu

D_MODEL = 2048
BATCH = 4
SEQ = 2048
DEC_BATCH = 32
DEC_SEQ = 8
N_A_LAYERS = 2
N_B_LAYERS = 2
DEPTH = 4
POOL_WINDOWS = (2, 4, 8, 16)
POOL_GROUP = D_MODEL // len(POOL_WINDOWS)
POOL_STATE = 15
HEAD_DIM = 64
N_HEADS = 32
N_KV_HEADS = 4
KV_DIM = N_KV_HEADS * HEAD_DIM
GROUP_LANES = D_MODEL // N_KV_HEADS
WINDOW = 128
D_FF = 4 * D_MODEL
RMS_EPS = 1e-5

N_PROMPT = BATCH * SEQ
N_SAMPLE = DEC_BATCH * DEC_SEQ
N_ROWS = N_PROMPT + N_SAMPLE

ROW_TILE = 512
N_ROW_TILES = -(-N_ROWS // ROW_TILE)
TILES_PER_SEQ = SEQ // ROW_TILE
N_PROMPT_TILES = N_PROMPT // ROW_TILE
SAMPLE_BLOCK = N_PROMPT // N_SAMPLE
HALO = 16
MXU_HALO = 128
MXU_POOL_WINDOWS = (8, 16)
MLP_ROW_TILE = 768
MLP_FF_TILE = 1024
N_MLP_TILES = N_ROWS // MLP_ROW_TILE
MLP_SAMPLE_ROW0 = N_PROMPT - (N_MLP_TILES - 1) * MLP_ROW_TILE
assert N_MLP_TILES * MLP_ROW_TILE == N_ROWS and MLP_SAMPLE_ROW0 + N_SAMPLE == MLP_ROW_TILE
KEY_SLOTS = 2 * WINDOW
SAMPLE_UNROLL = 4
CONVERT_STEPS = 64

VMEM_LIMIT = 56 * 1024 * 1024

_BF16 = jnp.bfloat16
_F32 = jnp.float32


def _rms(x, g):
    ms = jnp.mean(x * x, axis=-1, keepdims=True)
    return (x * lax.rsqrt(ms + RMS_EPS)) * g


def _dot(a, b):
    return jnp.dot(a, b, preferred_element_type=_F32)


def _pool_band_matrices():
    t = np.arange(ROW_TILE)[:, None]
    r = np.arange(MXU_HALO + ROW_TILE)[None, :]
    mats = []
    for w in MXU_POOL_WINDOWS:
        band = ((r > t + MXU_HALO - w) & (r <= t + MXU_HALO)).astype(np.float32)
        mats.append(np.concatenate([band, band], axis=1))
    return jnp.asarray(np.stack(mats), dtype=_BF16)


def _pool_kernel(hp_ref, halo_ref, hs_ref, past_ref, g_ref, w_ref, sc_ref, band_ref,
                 out_ref, poolp_ref, pools_ref, ext_ref, ext3_ref):
    i = pl.program_id(0)
    g = g_ref[...]

    @pl.when(i == 0)
    def _zero_unused_lookback():
        ext_ref[0:MXU_HALO - HALO, :] = jnp.zeros((MXU_HALO - HALO, D_MODEL), _F32)

    @pl.when(i < N_PROMPT_TILES)
    def _prompt():
        x = hp_ref[...]
        u = _rms(x, g)
        first = (i % TILES_PER_SEQ) == 0
        halo = jnp.where(first, 0.0, _rms(halo_ref[...], g))
        ext_ref[MXU_HALO - HALO:MXU_HALO, :] = halo
        ext_ref[MXU_HALO:MXU_HALO + ROW_TILE, :] = u
        poolp_ref[0] = ext_ref[MXU_HALO + ROW_TILE - POOL_STATE:MXU_HALO + ROW_TILE, :]
        pos = (i % TILES_PER_SEQ) * ROW_TILE + lax.broadcasted_iota(jnp.int32, (ROW_TILE, 1), 0)
        groups = [slice(gi * POOL_GROUP, (gi + 1) * POOL_GROUP) for gi in range(len(POOL_WINDOWS))]
        mxu_sums = {}
        for k, w in enumerate(MXU_POOL_WINDOWS):
            e = ext_ref[:, groups[POOL_WINDOWS.index(w)]]
            hi = e.astype(_BF16)
            lo = (e - hi.astype(_F32)).astype(_BF16)
            mxu_sums[w] = _dot(band_ref[k], jnp.concatenate([hi, lo], axis=0))
        for gi, w in enumerate(POOL_WINDOWS):
            sl = groups[gi]
            ug = u[:, sl]
            if w in mxu_sums:
                s = mxu_sums[w]
            else:
                s = ug
                for j in range(1, w):
                    s = s + ext_ref[MXU_HALO - j:MXU_HALO - j + ROW_TILE, sl]
            inv = 1.0 / jnp.minimum(pos + 1, w).astype(_F32)
            d = s * inv - ug
            o = _dot(d.astype(_BF16), w_ref[gi])
            out_ref[:, sl] = x[:, sl] + o * sc_ref[:, sl]

    @pl.when(i == N_PROMPT_TILES)
    def _sample():
        x = hs_ref[...]
        u = _rms(x, g)
        u3 = u.reshape(DEC_BATCH, DEC_SEQ, D_MODEL)
        for j in range(POOL_STATE):
            ext3_ref[:, HALO - POOL_STATE + j, :] = past_ref[j]
        ext3_ref[:, HALO:HALO + DEC_SEQ, :] = u3
        for j in range(POOL_STATE - DEC_SEQ):
            pools_ref[j] = past_ref[DEC_SEQ + j]
        for t in range(DEC_SEQ):
            pools_ref[POOL_STATE - DEC_SEQ + t] = ext3_ref[:, HALO + t, :]
        for gi, w in enumerate(POOL_WINDOWS):
            sl = slice(gi * POOL_GROUP, (gi + 1) * POOL_GROUP)
            ug = u3[:, :, sl]
            s = ug
            for j in range(1, w):
                s = s + ext3_ref[:, HALO - j:HALO - j + DEC_SEQ, sl]
            d = s * (1.0 / w) - ug
            o = _dot(d.reshape(N_SAMPLE, POOL_GROUP).astype(_BF16), w_ref[gi])
            out_ref[0:N_SAMPLE, sl] = x[:, sl] + o * sc_ref[:, sl]
        out_ref[N_SAMPLE:, :] = jnp.zeros((ROW_TILE - N_SAMPLE, D_MODEL), _F32)


def _pool_layer(layer, h_prompt, h_sample, sample_block, state_pool, norm_a, w_pool, pool_scale):
    halo_blocks_per_tile = ROW_TILE // HALO
    return pl.pallas_call(
        _pool_kernel,
        grid=(N_ROW_TILES,),
        in_specs=[
            pl.BlockSpec((ROW_TILE, D_MODEL), lambda i: (jnp.minimum(i, N_PROMPT_TILES - 1), 0)),
            pl.BlockSpec((HALO, D_MODEL),
                         lambda i: (jnp.clip(i * halo_blocks_per_tile - 1, 0,
                                             N_PROMPT // HALO - 1), 0)),
            pl.BlockSpec((N_SAMPLE, D_MODEL), lambda i: (sample_block, 0)),
            pl.BlockSpec((None, POOL_STATE, DEC_BATCH, D_MODEL), lambda i: (layer, 0, 0, 0),
                         pipeline_mode=pl.Buffered(1)),
            pl.BlockSpec((None, 1, D_MODEL), lambda i: (layer, 0, 0)),
            pl.BlockSpec((None, len(POOL_WINDOWS), POOL_GROUP, POOL_GROUP),
                         lambda i: (layer, 0, 0, 0)),
            pl.BlockSpec((None, 1, D_MODEL), lambda i: (layer, 0, 0)),
            pl.BlockSpec((len(MXU_POOL_WINDOWS), ROW_TILE, 2 * (MXU_HALO + ROW_TILE)),
                         lambda i: (0, 0, 0)),
        ],
        out_specs=[
            pl.BlockSpec((ROW_TILE, D_MODEL), lambda i: (i, 0)),
            pl.BlockSpec((1, POOL_STATE, D_MODEL),
                         lambda i: (jnp.minimum(i // TILES_PER_SEQ, BATCH - 1), 0, 0)),
            pl.BlockSpec((POOL_STATE, DEC_BATCH, D_MODEL), lambda i: (0, 0, 0)),
        ],
        out_shape=[
            jax.ShapeDtypeStruct((N_ROWS, D_MODEL), _F32),
            jax.ShapeDtypeStruct((BATCH, POOL_STATE, D_MODEL), _F32),
            jax.ShapeDtypeStruct((POOL_STATE, DEC_BATCH, D_MODEL), _F32),
        ],
        scratch_shapes=[
            pltpu.VMEM((MXU_HALO + ROW_TILE, D_MODEL), _F32),
            pltpu.VMEM((DEC_BATCH, HALO + DEC_SEQ, D_MODEL), _F32),
        ],
        compiler_params=pltpu.CompilerParams(
            dimension_semantics=("arbitrary",), vmem_limit_bytes=VMEM_LIMIT),
        name="pool_layer",
    )(h_prompt, h_prompt, h_sample, state_pool, norm_a, w_pool, pool_scale, _pool_band_matrices())


def _mlp_kernel(*refs, final, n_convert):
    h_ref, g_ref, wup_ref, wdn_ref = refs[:4]
    n_in = 4 + int(final) + n_convert
    gf_ref = refs[4] if final else None
    convert_src = refs[n_in - n_convert:n_in]
    out_ref = refs[n_in]
    outs_ref = refs[n_in + 1] if final else None
    n_out = 1 + int(final)
    convert_dst = refs[n_in + n_out:n_in + n_out + n_convert]
    hn_ref = refs[n_in + n_out + n_convert]
    i = pl.program_id(0)
    f = pl.program_id(1)

    @pl.when(f == 0)
    def _init():
        x = h_ref[...]
        hn_ref[...] = _rms(x, g_ref[...]).astype(_BF16)
        out_ref[...] = x

    a = jnp.maximum(_dot(hn_ref[...], wup_ref[...]), 0.0)
    out_ref[...] += _dot((a * a).astype(_BF16), wdn_ref[...])

    for src, dst in zip(convert_src, convert_dst):
        dst[...] = src[...].astype(_BF16)

    if final:
        last_f = f == pl.num_programs(1) - 1

        @pl.when(last_f)
        def _finish():
            out_ref[...] = _rms(out_ref[...], gf_ref[...])

        @pl.when(last_f & (i == N_MLP_TILES - 1))
        def _split():
            outs_ref[...] = out_ref[MLP_SAMPLE_ROW0:, :]


def _mlp_layer(layer, h, norm_mlp, w_up, w_down, norm_f=None, convert=()):
    final = norm_f is not None
    n_ff = D_FF // MLP_FF_TILE
    assert N_MLP_TILES * n_ff >= CONVERT_STEPS
    in_specs = [
        pl.BlockSpec((MLP_ROW_TILE, D_MODEL), lambda i, f: (i, 0)),
        pl.BlockSpec((None, 1, D_MODEL), lambda i, f: (layer, 0, 0)),
        pl.BlockSpec((D_MODEL, MLP_FF_TILE), lambda i, f: (0, f)),
        pl.BlockSpec((MLP_FF_TILE, D_MODEL), lambda i, f: (f, 0)),
    ]
    args = [h, norm_mlp, w_up, w_down]
    row_spec = pl.BlockSpec((MLP_ROW_TILE, D_MODEL), lambda i, f: (i, 0))
    if final:
        in_specs.append(pl.BlockSpec((1, D_MODEL), lambda i, f: (0, 0)))
        args.append(norm_f)
        out_specs = [row_spec, pl.BlockSpec((N_SAMPLE, D_MODEL), lambda i, f: (0, 0))]
        out_shape = [jax.ShapeDtypeStruct((N_PROMPT, D_MODEL), _F32),
                     jax.ShapeDtypeStruct((N_SAMPLE, D_MODEL), _F32)]
    else:
        out_specs = [row_spec]
        out_shape = [jax.ShapeDtypeStruct((N_ROWS, D_MODEL), _F32)]
    band = lambda i, f: jnp.minimum(i * n_ff + f, CONVERT_STEPS - 1)
    for stacked, idx in convert:
        _, rows, cols = stacked.shape
        rb = rows // CONVERT_STEPS
        assert rb * CONVERT_STEPS == rows and rb % 16 == 0
        in_specs.append(pl.BlockSpec((None, rb, cols), lambda i, f, idx=idx: (idx, band(i, f), 0)))
        out_specs.append(pl.BlockSpec((rb, cols), lambda i, f: (band(i, f), 0)))
        out_shape.append(jax.ShapeDtypeStruct((rows, cols), _BF16))
        args.append(stacked)
    return pl.pallas_call(
        functools.partial(_mlp_kernel, final=final, n_convert=len(convert)),
        grid=(N_MLP_TILES, n_ff),
        in_specs=in_specs,
        out_specs=out_specs,
        out_shape=out_shape,
        scratch_shapes=[pltpu.VMEM((MLP_ROW_TILE, D_MODEL), _BF16)],
        compiler_params=pltpu.CompilerParams(
            dimension_semantics=("arbitrary", "arbitrary"), vmem_limit_bytes=VMEM_LIMIT),
        name="mlp_layer",
    )(*args)


def _kv_kernel(h_ref, g_ref, wk_ref, wv_ref, out_ref):
    hk = _rms(h_ref[...], g_ref[...]).astype(_BF16)
    out_ref[:, 0:KV_DIM] = _dot(hk, wk_ref[...].astype(_BF16))
    out_ref[:, KV_DIM:] = _dot(hk, wv_ref[...].astype(_BF16))


def _kv_proj(h, gain, w_k, w_v):
    return pl.pallas_call(
        _kv_kernel,
        grid=(N_MLP_TILES,),
        in_specs=[
            pl.BlockSpec((MLP_ROW_TILE, D_MODEL), lambda i: (i, 0)),
            pl.BlockSpec((1, D_MODEL), lambda i: (0, 0)),
            pl.BlockSpec((D_MODEL, KV_DIM), lambda i: (0, 0)),
            pl.BlockSpec((D_MODEL, KV_DIM), lambda i: (0, 0)),
        ],
        out_specs=pl.BlockSpec((MLP_ROW_TILE, 2 * KV_DIM), lambda i: (i, 0)),
        out_shape=jax.ShapeDtypeStruct((N_ROWS, 2 * KV_DIM), _F32),
        compiler_params=pltpu.CompilerParams(
            dimension_semantics=("arbitrary",), vmem_limit_bytes=VMEM_LIMIT),
        name="kv_proj",
    )(h, gain, w_k, w_v)


def _pair_diag_tall(x_full, g):
    keys = x_full.shape[0]
    c, e0 = divmod(g, 2)
    xa = x_full[:, c * 2 * HEAD_DIM:(c + 1) * 2 * HEAD_DIM]
    xr = pltpu.roll(xa, HEAD_DIM, 1)
    low_half = lax.broadcasted_iota(jnp.int32, (keys, 2 * HEAD_DIM), 1) < HEAD_DIM
    top = jnp.where(low_half, xa if e0 == 0 else xr, 0.0)
    bot = jnp.where(low_half, 0.0, xr if e0 == 0 else xa)
    return jnp.concatenate([top, bot], axis=0).astype(_BF16)


def _pair_diag_wide(xt, g):
    xg = xt[g * HEAD_DIM:(g + 1) * HEAD_DIM, :]
    z = jnp.zeros_like(xg)
    return jnp.concatenate([jnp.concatenate([xg, z], axis=1),
                            jnp.concatenate([z, xg], axis=1)], axis=0)


def _rows_scores(q_groups, k_full):
    kt = k_full.T.astype(_BF16)
    scores = []
    for g in range(N_KV_HEADS):
        qg = q_groups[g]
        qs = jnp.concatenate([qg[:, p * 128:(p + 1) * 128] for p in range(4)], axis=0)
        scores.append(_dot(qs.astype(_BF16), _pair_diag_wide(kt, g)))
    return scores


def _rows_softmax(scores, allowed, tq, sink_ref):
    probs = []
    for g in range(N_KV_HEADS):
        s = scores[g]
        p_rows = []
        for p in range(4):
            p_cols = []
            for e in range(2):
                sink = sink_ref[g * 8 + p * 2 + e]
                spe = s[p * tq:(p + 1) * tq, e * KEY_SLOTS:(e + 1) * KEY_SLOTS]
                spe = jnp.where(allowed, spe, -jnp.inf)
                m = jnp.maximum(jnp.max(spe, axis=-1, keepdims=True), sink)
                pe = jnp.exp(spe - m)
                l = jnp.sum(pe, axis=-1, keepdims=True) + jnp.exp(sink - m)
                p_cols.append(pe * (1.0 / l))
            p_rows.append(jnp.concatenate(p_cols, axis=1))
        probs.append(jnp.concatenate(p_rows, axis=0).astype(_BF16))
    return probs


def _rows_values(probs, v_full, tq):
    outs = []
    for g in range(N_KV_HEADS):
        o = _dot(probs[g], _pair_diag_tall(v_full, g))
        outs.append(jnp.concatenate([o[p * tq:(p + 1) * tq, :] for p in range(4)], axis=1))
    return outs


def _attn_sample_rows(sink_ref, x, kv_ref, ck_ref, cv_ref, g_ref, wq_ref, wo_ref, q_ref, o_ref):
    hn = _rms(x, g_ref[...]).astype(_BF16)
    q = _dot(hn, wq_ref[...]) * (HEAD_DIM ** -0.5)
    for g in range(N_KV_HEADS):
        q_ref[g] = q[:, g * GROUP_LANES:(g + 1) * GROUP_LANES]

    qi = lax.broadcasted_iota(jnp.int32, (DEC_SEQ, KEY_SLOTS), 0)
    kj = lax.broadcasted_iota(jnp.int32, (DEC_SEQ, KEY_SLOTS), 1)
    allowed = ((kj < WINDOW) & (kj > qi)) | ((kj >= WINDOW) & (kj - WINDOW <= qi))
    pad = jnp.zeros((KEY_SLOTS - WINDOW - DEC_SEQ, KV_DIM), _F32)

    def body(j, carry):
        seqs = []
        for u in range(SAMPLE_UNROLL):
            b = j * SAMPLE_UNROLL + u
            rows = pl.ds(pl.multiple_of(b * DEC_SEQ, DEC_SEQ), DEC_SEQ)
            kb = jnp.concatenate([ck_ref[b], kv_ref[rows, 0:KV_DIM], pad], axis=0)
            scores = _rows_scores([q_ref[g, rows, :] for g in range(N_KV_HEADS)], kb)
            seqs.append((b, rows, scores))
        probs = [_rows_softmax(scores, allowed, DEC_SEQ, sink_ref) for _, _, scores in seqs]
        for (b, rows, _), pm in zip(seqs, probs):
            vb = jnp.concatenate([cv_ref[b], kv_ref[rows, KV_DIM:2 * KV_DIM], pad], axis=0)
            outs = _rows_values(pm, vb, DEC_SEQ)
            for g in range(N_KV_HEADS):
                o_ref[g, rows, :] = outs[g]
        return carry

    lax.fori_loop(0, DEC_BATCH // SAMPLE_UNROLL, body, 0)

    acc = x
    for g in range(N_KV_HEADS):
        acc = acc + _dot(o_ref[g].astype(_BF16), wo_ref[g * GROUP_LANES:(g + 1) * GROUP_LANES, :])
    return acc


def _attn_core_lanes(qt_ref, ot_ref, cols, k_full, v_full, allowed_t, sink_ref, between):
    vt = v_full.T.astype(_BF16)

    def scores(g):
        base = g * GROUP_LANES
        qst = jnp.concatenate([qt_ref[base + p * 128:base + (p + 1) * 128, cols]
                               for p in range(4)], axis=1)
        return _dot(_pair_diag_tall(k_full, g), qst)

    def softmax(g, st):
        inv = {}
        pt_rows = []
        for e in range(2):
            pt_cols = []
            for p in range(4):
                sink = sink_ref[g * 8 + p * 2 + e]
                sub = st[e * KEY_SLOTS:(e + 1) * KEY_SLOTS, p * WINDOW:(p + 1) * WINDOW]
                sub = jnp.where(allowed_t, sub, -jnp.inf)
                m = jnp.maximum(jnp.max(sub, axis=0, keepdims=True), sink)
                pe = jnp.exp(sub - m)
                l = jnp.sum(pe, axis=0, keepdims=True) + jnp.exp(sink - m)
                inv[e, p] = 1.0 / l
                pt_cols.append(pe.astype(_BF16))
            pt_rows.append(jnp.concatenate(pt_cols, axis=1))
        return jnp.concatenate(pt_rows, axis=0), inv

    def values(g, pt, inv):
        base = g * GROUP_LANES
        ot = _dot(_pair_diag_wide(vt, g), pt)
        for e in range(2):
            for p in range(4):
                r0 = base + p * 128 + e * HEAD_DIM
                ot_ref[r0:r0 + HEAD_DIM, cols] = (
                    ot[e * HEAD_DIM:(e + 1) * HEAD_DIM, p * WINDOW:(p + 1) * WINDOW] * inv[e, p])

    st = {0: scores(0), 1: scores(1)}
    between()
    for g in range(N_KV_HEADS):
        pt, inv = softmax(g, st.pop(g))
        if g + 2 < N_KV_HEADS:
            st[g + 2] = scores(g + 2)
        values(g, pt, inv)


def _attn_kernel(sink_ref, h_ref, kv_ref, kvp_ref, ck_ref, cv_ref, g_ref, wq_ref, wo_ref,
                 out_ref, qt_ref, ot_ref, q_ref, o_ref):
    i = pl.program_id(0)

    @pl.when(i < N_PROMPT_TILES)
    def _prompt():
        n_blocks = ROW_TILE // WINDOW
        cols = [slice(n * WINDOW, (n + 1) * WINDOW) for n in range(n_blocks)]

        def project_q(n):
            hn = _rms(h_ref[cols[n], :], g_ref[...]).astype(_BF16)
            q = _dot(hn, wq_ref[...]) * (HEAD_DIM ** -0.5)
            qt_ref[:, cols[n]] = q.T.astype(_BF16)

        def project_o(n):
            o = ot_ref[:, cols[n]].T.astype(_BF16)
            out_ref[cols[n], :] = h_ref[cols[n], :] + _dot(o, wo_ref[...])

        def between_blocks(n):
            if n > 0:
                project_o(n - 1)
            if n + 1 < n_blocks:
                project_q(n + 1)

        first = (i % TILES_PER_SEQ) == 0
        kj = lax.broadcasted_iota(jnp.int32, (KEY_SLOTS, WINDOW), 0)
        qi = lax.broadcasted_iota(jnp.int32, (KEY_SLOTS, WINDOW), 1)
        band = (kj > qi) & (kj <= qi + WINDOW)
        project_q(0)
        for n in range(n_blocks):
            allowed_t = band
            if n == 0:
                allowed_t = band & ((kj >= WINDOW) | jnp.logical_not(first))
                kv_prev = kvp_ref[...]
            else:
                kv_prev = kv_ref[(n - 1) * WINDOW:n * WINDOW, :]
            kv_blk = jnp.concatenate([kv_prev, kv_ref[cols[n], :]], axis=0)
            _attn_core_lanes(qt_ref, ot_ref, cols[n], kv_blk[:, 0:KV_DIM], kv_blk[:, KV_DIM:],
                             allowed_t, sink_ref, functools.partial(between_blocks, n))
        project_o(n_blocks - 1)

    @pl.when(i == N_PROMPT_TILES)
    def _sample():
        out_ref[0:N_SAMPLE, :] = _attn_sample_rows(
            sink_ref, h_ref[0:N_SAMPLE, :], kv_ref, ck_ref, cv_ref, g_ref, wq_ref, wo_ref, q_ref, o_ref)
        out_ref[N_SAMPLE:, :] = jnp.zeros((ROW_TILE - N_SAMPLE, D_MODEL), _F32)


def _attn_layer(layer, h, kv, cache_k, cache_v, norm_b, w_q, w_o, sinks):
    kv_blocks_per_tile = ROW_TILE // WINDOW
    const1 = pl.Buffered(1)
    return pl.pallas_call(
        _attn_kernel,
        grid=(N_ROW_TILES,),
        in_specs=[
            pl.BlockSpec(memory_space=pltpu.SMEM),
            pl.BlockSpec((ROW_TILE, D_MODEL), lambda i: (i, 0)),
            pl.BlockSpec((ROW_TILE, 2 * KV_DIM), lambda i: (i, 0)),
            pl.BlockSpec((WINDOW, 2 * KV_DIM),
                         lambda i: (jnp.maximum(i * kv_blocks_per_tile - 1, 0), 0)),
            pl.BlockSpec((DEC_BATCH, WINDOW, KV_DIM), lambda i: (0, 0, 0), pipeline_mode=const1),
            pl.BlockSpec((DEC_BATCH, WINDOW, KV_DIM), lambda i: (0, 0, 0), pipeline_mode=const1),
            pl.BlockSpec((None, 1, D_MODEL), lambda i: (layer, 0, 0)),
            pl.BlockSpec((D_MODEL, D_MODEL), lambda i: (0, 0), pipeline_mode=const1),
            pl.BlockSpec((D_MODEL, D_MODEL), lambda i: (0, 0), pipeline_mode=const1),
        ],
        out_specs=pl.BlockSpec((ROW_TILE, D_MODEL), lambda i: (i, 0)),
        out_shape=jax.ShapeDtypeStruct((N_ROWS, D_MODEL), _F32),
        scratch_shapes=[
            pltpu.VMEM((D_MODEL, ROW_TILE), _BF16),
            pltpu.VMEM((D_MODEL, ROW_TILE), _F32),
            pltpu.VMEM((N_KV_HEADS, N_SAMPLE, GROUP_LANES), _F32),
            pltpu.VMEM((N_KV_HEADS, N_SAMPLE, GROUP_LANES), _F32),
        ],
        compiler_params=pltpu.CompilerParams(
            dimension_semantics=("arbitrary",), vmem_limit_bytes=VMEM_LIMIT),
        name="attn_layer",
    )(sinks, h, kv, kv, cache_k, cache_v, norm_b, w_q, w_o)


def kernel(x_prompt, x_sample, state_pool, cache_k_win, cache_v_win, norm_a, w_pool, pool_scale,
           norm_kv, w_k, w_v, norm_b, w_q, w_o, sinks, norm_mlp, w_up, w_down, norm_f):
    cache_k = cache_k_win.reshape(DEC_BATCH, WINDOW, KV_DIM)
    cache_v = cache_v_win.reshape(DEC_BATCH, WINDOW, KV_DIM)
    rows3 = lambda v: v.reshape(v.shape[0], 1, D_MODEL)
    norm_a3, scale3, norm_b3, norm_mlp3 = rows3(norm_a), rows3(pool_scale), rows3(norm_b), rows3(norm_mlp)
    w_pool_b = w_pool.astype(_BF16)
    w_up_b, w_down_b = w_up[0].astype(_BF16), w_down[0].astype(_BF16)
    w_q_b = w_o_b = kv = None

    pool_p, pool_s = [], []
    state_pool_t = state_pool.transpose(0, 2, 1, 3)
    h_prompt = x_prompt.reshape(N_PROMPT, D_MODEL)
    h_sample, sample_block = x_sample.reshape(N_SAMPLE, D_MODEL), 0
    for l in range(DEPTH):
        if l < N_A_LAYERS:
            h, pp, ps = _pool_layer(l, h_prompt, h_sample, sample_block, state_pool_t, norm_a3,
                                    w_pool_b, scale3)
            pool_p.append(pp)
            pool_s.append(ps)
        else:
            h = _attn_layer(l - N_A_LAYERS, h, kv, cache_k, cache_v, norm_b3, w_q_b, w_o_b,
                            sinks[l - N_A_LAYERS])
        if l == DEPTH - 1:
            y_prompt, y_sample = _mlp_layer(l, h, norm_mlp3, w_up_b, w_down_b,
                                            norm_f=norm_f.reshape(1, D_MODEL))
            break
        convert = [(w_up, l + 1), (w_down, l + 1)]
        if l + 1 >= N_A_LAYERS:
            convert += [(w_q, l + 1 - N_A_LAYERS), (w_o, l + 1 - N_A_LAYERS)]
        h, w_up_b, w_down_b, *w_attn = _mlp_layer(l, h, norm_mlp3, w_up_b, w_down_b, convert=convert)
        if w_attn:
            w_q_b, w_o_b = w_attn
        h_prompt, h_sample, sample_block = h, h, SAMPLE_BLOCK
        if l == N_A_LAYERS - 1:
            kv = _kv_proj(h, norm_kv.reshape(1, D_MODEL), w_k, w_v)

    kv_tail = jnp.stack([kv[(b + 1) * SEQ - WINDOW:(b + 1) * SEQ] for b in range(BATCH)])
    kv_new = kv[N_PROMPT:]

    def prompt_window(a):
        return a.reshape(BATCH, WINDOW, N_KV_HEADS, HEAD_DIM)

    def sample_window(buf, a):
        new = a.reshape(DEC_BATCH, DEC_SEQ, N_KV_HEADS, HEAD_DIM)
        return jnp.concatenate([buf[:, DEC_SEQ:], new], axis=1)

    return (y_prompt.reshape(BATCH, SEQ, D_MODEL), y_sample.reshape(DEC_BATCH, DEC_SEQ, D_MODEL),
            jnp.stack(pool_p), jnp.stack(pool_s).transpose(0, 2, 1, 3),
            prompt_window(kv_tail[..., :KV_DIM]), prompt_window(kv_tail[..., KV_DIM:]),
            sample_window(cache_k_win, kv_new[:, :KV_DIM]), sample_window(cache_v_win, kv_new[:, KV_DIM:]))
```

```python
import functools

import jax
import jax.numpy as jnp
import numpy as np
from jax import lax
from jax.experimental import pallas as pl
from jax.experimental.pallas import tpu as pltpu

D_MODEL = 2048
BATCH = 4
SEQ = 2048
DEC_BATCH = 32
DEC_SEQ = 8
N_A_LAYERS = 2
N_B_LAYERS = 2
DEPTH = 4
POOL_WINDOWS = (2, 4, 8, 16)
POOL_GROUP = D_MODEL // len(POOL_WINDOWS)
POOL_STATE = 15
HEAD_DIM = 64
N_HEADS = 32
N_KV_HEADS = 4
KV_DIM = N_KV_HEADS * HEAD_DIM
GQA_GROUP = N_HEADS // N_KV_HEADS
GROUP_LANES = D_MODEL // N_KV_HEADS
PAIR_LANES = 2 * HEAD_DIM
PAIRS_PER_GROUP = GQA_GROUP // 2
WINDOW = 128
D_FF = 4 * D_MODEL
RMS_EPS = 1e-5

N_PROMPT = BATCH * SEQ
N_SAMPLE = DEC_BATCH * DEC_SEQ
N_ROWS = N_PROMPT + N_SAMPLE

ROW_TILE = 512
N_ROW_TILES = -(-N_ROWS // ROW_TILE)
TILES_PER_SEQ = SEQ // ROW_TILE
N_PROMPT_TILES = N_PROMPT // ROW_TILE
SAMPLE_BLOCK = N_PROMPT // N_SAMPLE
HALO = 16
MXU_HALO = 128
MXU_POOL_WINDOWS = (8, 16)
MLP_ROW_TILE = 768
MLP_FF_TILE = 1024
KV_ROW_TILE = 1056
assert N_ROWS % KV_ROW_TILE == 0
N_MLP_TILES = N_ROWS // MLP_ROW_TILE
MLP_SAMPLE_ROW0 = N_PROMPT - (N_MLP_TILES - 1) * MLP_ROW_TILE
assert N_MLP_TILES * MLP_ROW_TILE == N_ROWS and MLP_SAMPLE_ROW0 + N_SAMPLE == MLP_ROW_TILE
KEY_SLOTS = 2 * WINDOW
SAMPLE_UNROLL = 4
CONVERT_STEPS = 64

VMEM_LIMIT = 56 * 1024 * 1024
ATTN_VMEM_LIMIT = 58 * 1024 * 1024

_BF16 = jnp.bfloat16
_F32 = jnp.float32


def _rms(x, g):
    ms = jnp.mean(x * x, axis=-1, keepdims=True)
    return (x * lax.rsqrt(ms + RMS_EPS)) * g


def _dot(a, b):
    return jnp.dot(a, b, preferred_element_type=_F32)


def _pool_band_matrices():
    t = np.arange(ROW_TILE)[:, None]
    r = np.arange(MXU_HALO + ROW_TILE)[None, :]
    mats = []
    for w in MXU_POOL_WINDOWS:
        band = ((r > t + MXU_HALO - w) & (r <= t + MXU_HALO)).astype(np.float32)
        mats.append(np.concatenate([band, band], axis=1))
    return jnp.asarray(np.stack(mats), dtype=_BF16)


def _pool_kernel(hp_ref, halo_ref, hs_ref, past_ref, g_ref, w_ref, sc_ref, band_ref,
                 out_ref, poolp_ref, pools_ref, ext_ref, ext3_ref):
    i = pl.program_id(0)
    g = g_ref[...]

    @pl.when(i == 0)
    def _zero_unused_lookback():
        ext_ref[0:MXU_HALO - HALO, :] = jnp.zeros((MXU_HALO - HALO, D_MODEL), _F32)

    @pl.when(i < N_PROMPT_TILES)
    def _prompt():
        x = hp_ref[...]
        u = _rms(x, g)
        first = (i % TILES_PER_SEQ) == 0
        halo = jnp.where(first, 0.0, _rms(halo_ref[...], g))
        ext_ref[MXU_HALO - HALO:MXU_HALO, :] = halo
        ext_ref[MXU_HALO:MXU_HALO + ROW_TILE, :] = u
        poolp_ref[0] = ext_ref[MXU_HALO + ROW_TILE - POOL_STATE:MXU_HALO + ROW_TILE, :]
        pos = (i % TILES_PER_SEQ) * ROW_TILE + lax.broadcasted_iota(jnp.int32, (ROW_TILE, 1), 0)
        groups = [slice(gi * POOL_GROUP, (gi + 1) * POOL_GROUP) for gi in range(len(POOL_WINDOWS))]
        mxu_sums = {}
        for k, w in enumerate(MXU_POOL_WINDOWS):
            e = ext_ref[:, groups[POOL_WINDOWS.index(w)]]
            hi = e.astype(_BF16)
            lo = (e - hi.astype(_F32)).astype(_BF16)
            mxu_sums[w] = _dot(band_ref[k], jnp.concatenate([hi, lo], axis=0))
        for gi, w in enumerate(POOL_WINDOWS):
            sl = groups[gi]
            ug = u[:, sl]
            if w in mxu_sums:
                s = mxu_sums[w]
            else:
                s = ug
                for j in range(1, w):
                    s = s + ext_ref[MXU_HALO - j:MXU_HALO - j + ROW_TILE, sl]
            inv = 1.0 / jnp.minimum(pos + 1, w).astype(_F32)
            d = s * inv - ug
            o = _dot(d.astype(_BF16), w_ref[gi].astype(_BF16))
            out_ref[:, sl] = x[:, sl] + o * sc_ref[:, sl]

    @pl.when(i == N_PROMPT_TILES)
    def _sample():
        x = hs_ref[...]
        u = _rms(x, g)
        u3 = u.reshape(DEC_BATCH, DEC_SEQ, D_MODEL)
        for j in range(POOL_STATE):
            ext3_ref[:, HALO - POOL_STATE + j, :] = past_ref[j]
        ext3_ref[:, HALO:HALO + DEC_SEQ, :] = u3
        for j in range(POOL_STATE - DEC_SEQ):
            pools_ref[j] = past_ref[DEC_SEQ + j]
        for t in range(DEC_SEQ):
            pools_ref[POOL_STATE - DEC_SEQ + t] = ext3_ref[:, HALO + t, :]
        for gi, w in enumerate(POOL_WINDOWS):
            sl = slice(gi * POOL_GROUP, (gi + 1) * POOL_GROUP)
            ug = u3[:, :, sl]
            s = ug
            for j in range(1, w):
                s = s + ext3_ref[:, HALO - j:HALO - j + DEC_SEQ, sl]
            d = s * (1.0 / w) - ug
            o = _dot(d.reshape(N_SAMPLE, POOL_GROUP).astype(_BF16), w_ref[gi].astype(_BF16))
            out_ref[0:N_SAMPLE, sl] = x[:, sl] + o * sc_ref[:, sl]
        out_ref[N_SAMPLE:, :] = jnp.zeros((ROW_TILE - N_SAMPLE, D_MODEL), _F32)


def _pool_layer(layer, h_prompt, h_sample, sample_block, state_pool, norm_a, w_pool, pool_scale):
    halo_blocks_per_tile = ROW_TILE // HALO
    return pl.pallas_call(
        _pool_kernel,
        grid=(N_ROW_TILES,),
        in_specs=[
            pl.BlockSpec((ROW_TILE, D_MODEL), lambda i: (jnp.minimum(i, N_PROMPT_TILES - 1), 0)),
            pl.BlockSpec((HALO, D_MODEL),
                         lambda i: (jnp.clip(i * halo_blocks_per_tile - 1, 0,
                                             N_PROMPT // HALO - 1), 0)),
            pl.BlockSpec((N_SAMPLE, D_MODEL), lambda i: (sample_block, 0)),
            pl.BlockSpec((None, POOL_STATE, DEC_BATCH, D_MODEL), lambda i: (layer, 0, 0, 0),
                         pipeline_mode=pl.Buffered(1)),
            pl.BlockSpec((None, 1, D_MODEL), lambda i: (layer, 0, 0)),
            pl.BlockSpec((None, len(POOL_WINDOWS), POOL_GROUP, POOL_GROUP),
                         lambda i: (layer, 0, 0, 0), pipeline_mode=pl.Buffered(1)),
            pl.BlockSpec((None, 1, D_MODEL), lambda i: (layer, 0, 0)),
            pl.BlockSpec((len(MXU_POOL_WINDOWS), ROW_TILE, 2 * (MXU_HALO + ROW_TILE)),
                         lambda i: (0, 0, 0)),
        ],
        out_specs=[
            pl.BlockSpec((ROW_TILE, D_MODEL), lambda i: (i, 0)),
            pl.BlockSpec((1, POOL_STATE, D_MODEL),
                         lambda i: (jnp.minimum(i // TILES_PER_SEQ, BATCH - 1), 0, 0)),
            pl.BlockSpec((POOL_STATE, DEC_BATCH, D_MODEL), lambda i: (0, 0, 0)),
        ],
        out_shape=[
            jax.ShapeDtypeStruct((N_ROWS, D_MODEL), _F32),
            jax.ShapeDtypeStruct((BATCH, POOL_STATE, D_MODEL), _F32),
            jax.ShapeDtypeStruct((POOL_STATE, DEC_BATCH, D_MODEL), _F32),
        ],
        scratch_shapes=[
            pltpu.VMEM((MXU_HALO + ROW_TILE, D_MODEL), _F32),
            pltpu.VMEM((DEC_BATCH, HALO + DEC_SEQ, D_MODEL), _F32),
        ],
        compiler_params=pltpu.CompilerParams(
            dimension_semantics=("arbitrary",), vmem_limit_bytes=VMEM_LIMIT),
        name="pool_layer",
    )(h_prompt, h_prompt, h_sample, state_pool, norm_a, w_pool, pool_scale, _pool_band_matrices())


def _mlp_kernel(*refs, final, n_convert, hn_given):
    h_ref, g_ref, wup_ref, wdn_ref = refs[:4]
    n_in = 4 + int(final) + int(hn_given) + n_convert
    gf_ref = refs[4] if final else None
    convert_src = refs[n_in - n_convert:n_in]
    out_ref = refs[n_in]
    outs_ref = refs[n_in + 1] if final else None
    n_out = 1 + int(final)
    convert_dst = refs[n_in + n_out:n_in + n_out + n_convert]
    hn_ref = refs[4 + int(final)] if hn_given else refs[n_in + n_out + n_convert]
    i = pl.program_id(0)
    f = pl.program_id(1)

    @pl.when(f == 0)
    def _init():
        x = h_ref[...]
        if not hn_given:
            hn_ref[...] = _rms(x, g_ref[...]).astype(_BF16)
        out_ref[...] = x

    a = jnp.maximum(_dot(hn_ref[...], wup_ref[...]), 0.0)
    out_ref[...] += _dot((a * a).astype(_BF16), wdn_ref[...])

    if n_convert:
        @pl.when(i * pl.num_programs(1) + f < CONVERT_STEPS)
        def _convert():
            for src, dst in zip(convert_src, convert_dst):
                dst[...] = src[...].astype(_BF16)

    if final:
        last_f = f == pl.num_programs(1) - 1

        @pl.when(last_f)
        def _finish():
            out_ref[...] = _rms(out_ref[...], gf_ref[...])

        @pl.when(last_f & (i == N_MLP_TILES - 1))
        def _split():
            outs_ref[...] = out_ref[MLP_SAMPLE_ROW0:, :]


def _mlp_layer(layer, h, norm_mlp, w_up, w_down, norm_f=None, convert=(), hn=None):
    final = norm_f is not None
    hn_given = hn is not None
    n_ff = D_FF // MLP_FF_TILE
    assert N_MLP_TILES * n_ff >= CONVERT_STEPS
    in_specs = [
        pl.BlockSpec((MLP_ROW_TILE, D_MODEL), lambda i, f: (i, 0)),
        pl.BlockSpec((None, 1, D_MODEL), lambda i, f: (layer, 0, 0)),
        pl.BlockSpec((D_MODEL, MLP_FF_TILE), lambda i, f: (0, f)),
        pl.BlockSpec((MLP_FF_TILE, D_MODEL), lambda i, f: (f, 0)),
    ]
    args = [h, norm_mlp, w_up, w_down]
    row_spec = pl.BlockSpec((MLP_ROW_TILE, D_MODEL), lambda i, f: (i, 0))
    if final:
        in_specs.append(pl.BlockSpec((1, D_MODEL), lambda i, f: (0, 0)))
        args.append(norm_f)
        out_specs = [row_spec, pl.BlockSpec((N_SAMPLE, D_MODEL), lambda i, f: (0, 0))]
        out_shape = [jax.ShapeDtypeStruct((N_PROMPT, D_MODEL), _F32),
                     jax.ShapeDtypeStruct((N_SAMPLE, D_MODEL), _F32)]
    else:
        out_specs = [row_spec]
        out_shape = [jax.ShapeDtypeStruct((N_ROWS, D_MODEL), _F32)]
    if hn_given:
        in_specs.append(pl.BlockSpec((MLP_ROW_TILE, D_MODEL), lambda i, f: (i, 0)))
        args.append(hn)
    band = lambda i, f: jnp.minimum(i * n_ff + f, CONVERT_STEPS - 1)
    for stacked, idx in convert:
        _, rows, cols = stacked.shape
        rb = rows // CONVERT_STEPS
        assert rb * CONVERT_STEPS == rows and rb % 16 == 0
        in_specs.append(pl.BlockSpec((None, rb, cols), lambda i, f, idx=idx: (idx, band(i, f), 0)))
        out_specs.append(pl.BlockSpec((rb, cols), lambda i, f: (band(i, f), 0)))
        out_shape.append(jax.ShapeDtypeStruct((rows, cols), _BF16))
        args.append(stacked)
    return pl.pallas_call(
        functools.partial(_mlp_kernel, final=final, n_convert=len(convert), hn_given=hn_given),
        grid=(N_MLP_TILES, n_ff),
        in_specs=in_specs,
        out_specs=out_specs,
        out_shape=out_shape,
        scratch_shapes=[] if hn_given else [pltpu.VMEM((MLP_ROW_TILE, D_MODEL), _BF16)],
        compiler_params=pltpu.CompilerParams(
            dimension_semantics=("arbitrary", "arbitrary"), vmem_limit_bytes=VMEM_LIMIT),
        name="mlp_layer",
    )(*args)


def _kv_kernel(h_ref, g_ref, wk_ref, wv_ref, out_ref):
    hk = _rms(h_ref[...], g_ref[...]).astype(_BF16)
    out_ref[:, 0:KV_DIM] = _dot(hk, wk_ref[...].astype(_BF16))
    out_ref[:, KV_DIM:] = _dot(hk, wv_ref[...].astype(_BF16))


def _kv_proj(h, gain, w_k, w_v):
    return pl.pallas_call(
        _kv_kernel,
        grid=(N_ROWS // KV_ROW_TILE,),
        in_specs=[
            pl.BlockSpec((KV_ROW_TILE, D_MODEL), lambda i: (i, 0)),
            pl.BlockSpec((1, D_MODEL), lambda i: (0, 0)),
            pl.BlockSpec((D_MODEL, KV_DIM), lambda i: (0, 0)),
            pl.BlockSpec((D_MODEL, KV_DIM), lambda i: (0, 0)),
        ],
        out_specs=pl.BlockSpec((KV_ROW_TILE, 2 * KV_DIM), lambda i: (i, 0)),
        out_shape=jax.ShapeDtypeStruct((N_ROWS, 2 * KV_DIM), _F32),
        compiler_params=pltpu.CompilerParams(
            dimension_semantics=("arbitrary",), vmem_limit_bytes=VMEM_LIMIT),
        name="kv_proj",
    )(h, gain, w_k, w_v)


def _pair_diag_tall(x_full, g):
    keys = x_full.shape[0]
    c, e0 = divmod(g, 2)
    xa = x_full[:, c * 2 * HEAD_DIM:(c + 1) * 2 * HEAD_DIM]
    xr = pltpu.roll(xa, HEAD_DIM, 1)
    low_half = lax.broadcasted_iota(jnp.int32, (keys, 2 * HEAD_DIM), 1) < HEAD_DIM
    top = jnp.where(low_half, xa if e0 == 0 else xr, 0.0)
    bot = jnp.where(low_half, 0.0, xr if e0 == 0 else xa)
    return jnp.concatenate([top, bot], axis=0).astype(_BF16)


def _pair_diag_wide(xt, g):
    xg = xt[g * HEAD_DIM:(g + 1) * HEAD_DIM, :]
    z = jnp.zeros_like(xg)
    return jnp.concatenate([jnp.concatenate([xg, z], axis=1),
                            jnp.concatenate([z, xg], axis=1)], axis=0)


def _rows_scores(q_groups, k_full):
    kt = k_full.T.astype(_BF16)
    scores = []
    for g in range(N_KV_HEADS):
        qg = q_groups[g]
        qs = jnp.concatenate([qg[:, p * PAIR_LANES:(p + 1) * PAIR_LANES]
                              for p in range(PAIRS_PER_GROUP)], axis=0)
        scores.append(_dot(qs.astype(_BF16), _pair_diag_wide(kt, g)))
    return scores


def _rows_softmax(scores, allowed, tq, sink_ref):
    probs = []
    for g in range(N_KV_HEADS):
        s = scores[g]
        p_rows = []
        for p in range(PAIRS_PER_GROUP):
            p_cols = []
            for e in range(2):
                sink = sink_ref[g * GQA_GROUP + p * 2 + e]
                spe = s[p * tq:(p + 1) * tq, e * KEY_SLOTS:(e + 1) * KEY_SLOTS]
                spe = jnp.where(allowed, spe, -jnp.inf)
                m = jnp.maximum(jnp.max(spe, axis=-1, keepdims=True), sink)
                pe = jnp.exp(spe - m)
                l = jnp.sum(pe, axis=-1, keepdims=True) + jnp.exp(sink - m)
                p_cols.append(pe * (1.0 / l))
            p_rows.append(jnp.concatenate(p_cols, axis=1))
        probs.append(jnp.concatenate(p_rows, axis=0).astype(_BF16))
    return probs


def _rows_values(probs, v_full, tq):
    outs = []
    for g in range(N_KV_HEADS):
        o = _dot(probs[g], _pair_diag_tall(v_full, g))
        outs.append(jnp.concatenate([o[p * tq:(p + 1) * tq, :] for p in range(PAIRS_PER_GROUP)], axis=1))
    return outs


def _attn_sample_rows(sink_ref, x, kv_ref, ck_ref, cv_ref, g_ref, wq_ref, wo_ref, q_ref, o_ref):
    hn = _rms(x, g_ref[...]).astype(_BF16)
    q = _dot(hn, wq_ref[...]) * (HEAD_DIM ** -0.5)
    for g in range(N_KV_HEADS):
        q_ref[g] = q[:, g * GROUP_LANES:(g + 1) * GROUP_LANES]

    qi = lax.broadcasted_iota(jnp.int32, (DEC_SEQ, KEY_SLOTS), 0)
    kj = lax.broadcasted_iota(jnp.int32, (DEC_SEQ, KEY_SLOTS), 1)
    allowed = ((kj < WINDOW) & (kj > qi)) | ((kj >= WINDOW) & (kj - WINDOW <= qi))
    pad = jnp.zeros((KEY_SLOTS - WINDOW - DEC_SEQ, KV_DIM), _F32)

    def body(j, carry):
        seqs = []
        for u in range(SAMPLE_UNROLL):
            b = j * SAMPLE_UNROLL + u
            rows = pl.ds(pl.multiple_of(b * DEC_SEQ, DEC_SEQ), DEC_SEQ)
            kb = jnp.concatenate([ck_ref[b], kv_ref[rows, 0:KV_DIM], pad], axis=0)
            scores = _rows_scores([q_ref[g, rows, :] for g in range(N_KV_HEADS)], kb)
            seqs.append((b, rows, scores))
        probs = [_rows_softmax(scores, allowed, DEC_SEQ, sink_ref) for _, _, scores in seqs]
        for (b, rows, _), pm in zip(seqs, probs):
            vb = jnp.concatenate([cv_ref[b], kv_ref[rows, KV_DIM:2 * KV_DIM], pad], axis=0)
            outs = _rows_values(pm, vb, DEC_SEQ)
            for g in range(N_KV_HEADS):
                o_ref[g, rows, :] = outs[g]
        return carry

    lax.fori_loop(0, DEC_BATCH // SAMPLE_UNROLL, body, 0)

    acc = x
    for g in range(N_KV_HEADS):
        acc = acc + _dot(o_ref[g].astype(_BF16), wo_ref[g * GROUP_LANES:(g + 1) * GROUP_LANES, :])
    return acc


def _attn_core_lanes(qt_ref, ot_ref, cols, k_full, v_full, allowed_t, sink_ref, between):
    vt = v_full.T.astype(_BF16)

    def scores(g):
        base = g * GROUP_LANES
        qst = jnp.concatenate([qt_ref[base + p * PAIR_LANES:base + (p + 1) * PAIR_LANES, cols]
                               for p in range(PAIRS_PER_GROUP)], axis=1)
        return _dot(_pair_diag_tall(k_full, g), qst)

    def softmax(g, st):
        inv = {}
        pt_rows = []
        for e in range(2):
            pt_cols = []
            for p in range(PAIRS_PER_GROUP):
                sink = sink_ref[g * GQA_GROUP + p * 2 + e]
                sub = st[e * KEY_SLOTS:(e + 1) * KEY_SLOTS, p * WINDOW:(p + 1) * WINDOW]
                sub = jnp.where(allowed_t, sub, -jnp.inf)
                m = jnp.maximum(jnp.max(sub, axis=0, keepdims=True), sink)
                pe = jnp.exp(sub - m)
                l = jnp.sum(pe, axis=0, keepdims=True) + jnp.exp(sink - m)
                inv[e, p] = 1.0 / l
                pt_cols.append(pe.astype(_BF16))
            pt_rows.append(jnp.concatenate(pt_cols, axis=1))
        return jnp.concatenate(pt_rows, axis=0), inv

    def values(g, pt, inv):
        base = g * GROUP_LANES
        ot = _dot(_pair_diag_wide(vt, g), pt)
        for e in range(2):
            for p in range(PAIRS_PER_GROUP):
                r0 = base + p * PAIR_LANES + e * HEAD_DIM
                ot_ref[r0:r0 + HEAD_DIM, cols] = (
                    ot[e * HEAD_DIM:(e + 1) * HEAD_DIM, p * WINDOW:(p + 1) * WINDOW] * inv[e, p])

    st = {0: scores(0), 1: scores(1)}
    between()
    for g in range(N_KV_HEADS):
        pt, inv = softmax(g, st.pop(g))
        if g + 2 < N_KV_HEADS:
            st[g + 2] = scores(g + 2)
        values(g, pt, inv)


def _attn_kernel(sink_ref, h_ref, kv_ref, kvp_ref, ck_ref, cv_ref, g_ref, gm_ref, wq_ref, wo_ref,
                 out_ref, hn_ref, qt_ref, ot_ref, q_ref, o_ref):
    i = pl.program_id(0)

    @pl.when(i < N_PROMPT_TILES)
    def _prompt():
        n_blocks = ROW_TILE // WINDOW
        cols = [slice(n * WINDOW, (n + 1) * WINDOW) for n in range(n_blocks)]

        def project_q(n):
            hn = _rms(h_ref[cols[n], :], g_ref[...]).astype(_BF16)
            q = _dot(hn, wq_ref[...]) * (HEAD_DIM ** -0.5)
            qt_ref[:, cols[n]] = q.T.astype(_BF16)

        def project_o(n):
            o = ot_ref[:, cols[n]].T.astype(_BF16)
            y = h_ref[cols[n], :] + _dot(o, wo_ref[...])
            out_ref[cols[n], :] = y
            hn_ref[cols[n], :] = _rms(y, gm_ref[...]).astype(_BF16)

        def between_blocks(n):
            if n > 0:
                project_o(n - 1)
            if n + 1 < n_blocks:
                project_q(n + 1)

        first = (i % TILES_PER_SEQ) == 0
        kj = lax.broadcasted_iota(jnp.int32, (KEY_SLOTS, WINDOW), 0)
        qi = lax.broadcasted_iota(jnp.int32, (KEY_SLOTS, WINDOW), 1)
        band = (kj > qi) & (kj <= qi + WINDOW)
        project_q(0)
        for n in range(n_blocks):
            allowed_t = band
            if n == 0:
                allowed_t = band & ((kj >= WINDOW) | jnp.logical_not(first))
                kv_prev = kvp_ref[...]
            else:
                kv_prev = kv_ref[(n - 1) * WINDOW:n * WINDOW, :]
            kv_blk = jnp.concatenate([kv_prev, kv_ref[cols[n], :]], axis=0)
            _attn_core_lanes(qt_ref, ot_ref, cols[n], kv_blk[:, 0:KV_DIM], kv_blk[:, KV_DIM:],
                             allowed_t, sink_ref, functools.partial(between_blocks, n))
        project_o(n_blocks - 1)

    @pl.when(i == N_PROMPT_TILES)
    def _sample():
        y = _attn_sample_rows(
            sink_ref, h_ref[0:N_SAMPLE, :], kv_ref, ck_ref, cv_ref, g_ref, wq_ref, wo_ref, q_ref, o_ref)
        out_ref[0:N_SAMPLE, :] = y
        out_ref[N_SAMPLE:, :] = jnp.zeros((ROW_TILE - N_SAMPLE, D_MODEL), _F32)
        hn_ref[0:N_SAMPLE, :] = _rms(y, gm_ref[...]).astype(_BF16)
        hn_ref[N_SAMPLE:, :] = jnp.zeros((ROW_TILE - N_SAMPLE, D_MODEL), _BF16)


def _attn_layer(layer, mlp_layer, h, kv, cache_k, cache_v, norm_b, norm_mlp, w_q, w_o, sinks):
    kv_blocks_per_tile = ROW_TILE // WINDOW
    const1 = pl.Buffered(1)
    return pl.pallas_call(
        _attn_kernel,
        grid=(N_ROW_TILES,),
        in_specs=[
            pl.BlockSpec(memory_space=pltpu.SMEM),
            pl.BlockSpec((ROW_TILE, D_MODEL), lambda i: (i, 0)),
            pl.BlockSpec((ROW_TILE, 2 * KV_DIM), lambda i: (i, 0)),
            pl.BlockSpec((WINDOW, 2 * KV_DIM),
                         lambda i: (jnp.maximum(i * kv_blocks_per_tile - 1, 0), 0)),
            pl.BlockSpec((DEC_BATCH, WINDOW, KV_DIM), lambda i: (0, 0, 0), pipeline_mode=const1),
            pl.BlockSpec((DEC_BATCH, WINDOW, KV_DIM), lambda i: (0, 0, 0), pipeline_mode=const1),
            pl.BlockSpec((None, 1, D_MODEL), lambda i: (layer, 0, 0)),
            pl.BlockSpec((None, 1, D_MODEL), lambda i: (mlp_layer, 0, 0)),
            pl.BlockSpec((D_MODEL, D_MODEL), lambda i: (0, 0), pipeline_mode=const1),
            pl.BlockSpec((D_MODEL, D_MODEL), lambda i: (0, 0), pipeline_mode=const1),
        ],
        out_specs=[pl.BlockSpec((ROW_TILE, D_MODEL), lambda i: (i, 0)),
                   pl.BlockSpec((ROW_TILE, D_MODEL), lambda i: (i, 0))],
        out_shape=[jax.ShapeDtypeStruct((N_ROWS, D_MODEL), _F32),
                   jax.ShapeDtypeStruct((N_ROWS, D_MODEL), _BF16)],
        scratch_shapes=[
            pltpu.VMEM((D_MODEL, ROW_TILE), _BF16),
            pltpu.VMEM((D_MODEL, ROW_TILE), _F32),
            pltpu.VMEM((N_KV_HEADS, N_SAMPLE, GROUP_LANES), _F32),
            pltpu.VMEM((N_KV_HEADS, N_SAMPLE, GROUP_LANES), _F32),
        ],
        compiler_params=pltpu.CompilerParams(
            dimension_semantics=("arbitrary",), vmem_limit_bytes=ATTN_VMEM_LIMIT),
        name="attn_layer",
    )(sinks, h, kv, kv, cache_k, cache_v, norm_b, norm_mlp, w_q, w_o)


def kernel(x_prompt, x_sample, state_pool, cache_k_win, cache_v_win, norm_a, w_pool, pool_scale,
           norm_kv, w_k, w_v, norm_b, w_q, w_o, sinks, norm_mlp, w_up, w_down, norm_f):
    cache_k = cache_k_win.reshape(DEC_BATCH, WINDOW, KV_DIM)
    cache_v = cache_v_win.reshape(DEC_BATCH, WINDOW, KV_DIM)
    rows3 = lambda v: v.reshape(v.shape[0], 1, D_MODEL)
    norm_a3, scale3, norm_b3, norm_mlp3 = rows3(norm_a), rows3(pool_scale), rows3(norm_b), rows3(norm_mlp)
    w_up_b, w_down_b = w_up[0].astype(_BF16), w_down[0].astype(_BF16)
    attn_w = []
    kv = None

    pool_p, pool_s = [], []
    hn = None
    state_pool_t = state_pool.transpose(0, 2, 1, 3)
    h_prompt = x_prompt.reshape(N_PROMPT, D_MODEL)
    h_sample, sample_block = x_sample.reshape(N_SAMPLE, D_MODEL), 0
    for l in range(DEPTH):
        if l < N_A_LAYERS:
            h, pp, ps = _pool_layer(l, h_prompt, h_sample, sample_block, state_pool_t, norm_a3,
                                    w_pool, scale3)
            pool_p.append(pp)
            pool_s.append(ps)
        else:
            j = l - N_A_LAYERS
            h, hn = _attn_layer(j, l, h, kv, cache_k, cache_v, norm_b3, norm_mlp3,
                                attn_w[j][0], attn_w[j][1], sinks[j])
        if l == DEPTH - 1:
            y_prompt, y_sample = _mlp_layer(l, h, norm_mlp3, w_up_b, w_down_b,
                                            norm_f=norm_f.reshape(1, D_MODEL), hn=hn)
            break
        convert = [(w_up, l + 1), (w_down, l + 1)]
        if l == N_A_LAYERS - 1:
            for j in range(N_B_LAYERS):
                convert += [(w_q, j), (w_o, j)]
        h, w_up_b, w_down_b, *w_attn = _mlp_layer(l, h, norm_mlp3, w_up_b, w_down_b, convert=convert, hn=hn)
        if w_attn:
            attn_w = [tuple(w_attn[2 * j:2 * j + 2]) for j in range(N_B_LAYERS)]
        h_prompt, h_sample, sample_block = h, h, SAMPLE_BLOCK
        if l == N_A_LAYERS - 1:
            kv = _kv_proj(h, norm_kv.reshape(1, D_MODEL), w_k, w_v)

    kv_tail = jnp.stack([kv[(b + 1) * SEQ - WINDOW:(b + 1) * SEQ] for b in range(BATCH)])
    kv_new = kv[N_PROMPT:]

    def prompt_window(a):
        return a.reshape(BATCH, WINDOW, N_KV_HEADS, HEAD_DIM)

    def sample_window(buf, a):
        new = a.reshape(DEC_BATCH, DEC_SEQ, N_KV_HEADS, HEAD_DIM)
        return jnp.concatenate([buf[:, DEC_SEQ:], new], axis=1)

    return (y_prompt.reshape(BATCH, SEQ, D_MODEL), y_sample.reshape(DEC_BATCH, DEC_SEQ, D_MODEL),
            jnp.stack(pool_p), jnp.stack(pool_s).transpose(0, 2, 1, 3),
            prompt_window(kv_tail[..., :KV_DIM]), prompt_window(kv_tail[..., KV_DIM:]),
            sample_window(cache_k_win, kv_new[:, :KV_DIM]), sample_window(cache_v_win, kv_new[:, KV_DIM:]))
```
